```python
import jax, jax.numpy as jnp
from jax import lax
import numpy as np

D_MODEL = 1024
BATCH = 32
SEQ = 2048
DEPTH = 4
DEC_BATCH = 16
DEC_SEQ = 2048
PAST_LEN = 128

N_MIXERS = 3
N_LAYERS_A = (DEPTH + 2) // 3
N_LAYERS_B = (DEPTH + 1) // 3
N_LAYERS_C = DEPTH // 3

ROPE_THETA = 10000.0
NORM_EPS = 1e-6
NEG_BIG = -1e30

MLA_HEADS = 8
MLA_Q_LORA = 384
MLA_KV_LORA = 256
MLA_NOPE = 128
MLA_ROPE = 64
MLA_V = 128
MLA_QK = MLA_NOPE + MLA_ROPE
MLA_Q_BLOCK = 128

SWA_Q_HEADS = 16
SWA_KV_HEADS = 4
SWA_GROUP = SWA_Q_HEADS // SWA_KV_HEADS
SWA_HEAD_DIM = 64
SWA_HALF_WINDOW = 128
SWA_BLOCK = 128

DIL_CONFIGS = ((128, 1), (512, 4), (2048, 16))
DIL_GROUPS = len(DIL_CONFIGS)
DIL_HEADS_PER_GROUP = 8
DIL_HEAD_DIM = 64
DIL_BLOCK = 64

D_FF = -(-8 * D_MODEL // (3 * 256)) * 256

kernel_name = 'hybrid_mla_swa_dilated_encoder'


def rms_norm(x, g):
    xf = x.astype(jnp.float32)
    y = xf * lax.rsqrt(jnp.mean(xf * xf, axis=-1, keepdims=True) + NORM_EPS)
    return (y * g.astype(jnp.float32)).astype(x.dtype)


def rope_tables(seq, dim):
    inv = 1.0 / (ROPE_THETA ** (jnp.arange(0, dim, 2, dtype=jnp.float32) / dim))
    ang = jnp.arange(seq, dtype=jnp.float32)[:, None] * inv[None, :]
    return jnp.cos(ang), jnp.sin(ang)


def apply_rope(x, cos, sin):
    half = x.shape[-1] // 2
    xf = x.astype(jnp.float32)
    x1, x2 = xf[..., :half], xf[..., half:]
    c = cos[None, :, None, :]
    s = sin[None, :, None, :]
    return jnp.concatenate([x1 * c - x2 * s, x2 * c + x1 * s], axis=-1).astype(x.dtype)


def _pad_axis1(t, lo, hi):
    return jnp.pad(t, [(0, 0), (lo, hi)] + [(0, 0)] * (t.ndim - 2))


def banded_attention(q, k, v, half_window, block, sink=None):
    b, n, kvh, g, d = q.shape
    nb = -(-n // block)
    n_pad = nb * block
    qp = _pad_axis1(q, 0, n_pad - n)
    kp = _pad_axis1(k, block, n_pad - n + block)
    vp = _pad_axis1(v, block, n_pad - n + block)
    scale = d ** -0.5
    offs_q = jnp.arange(block)
    offs_k = jnp.arange(3 * block) - block

    def one_block(j):
        start = j * block
        qb = lax.dynamic_slice_in_dim(qp, start, block, axis=1).astype(jnp.float32)
        kb = lax.dynamic_slice_in_dim(kp, start, 3 * block, axis=1).astype(jnp.float32)
        vb = lax.dynamic_slice_in_dim(vp, start, 3 * block, axis=1).astype(jnp.float32)
        qi = start + offs_q
        ki = start + offs_k
        mask = (jnp.abs(qi[:, None] - ki[None, :]) <= half_window) & ((ki >= 0) & (ki < n))[None, :]
        s = jnp.einsum('bqhgd,bshd->bhgqs', qb, kb) * scale
        s = jnp.where(mask, s, NEG_BIG)
        m = jnp.max(s, axis=-1)
        if sink is not None:
            sk = sink.astype(jnp.float32)[None, :, :, None]
            m = jnp.maximum(m, sk)
        p = jnp.exp(s - m[..., None])
        den = jnp.sum(p, axis=-1)
        if sink is not None:
            den = den + jnp.exp(sk - m)
        den_t = jnp.transpose(den, (0, 3, 1, 2))
        o = jnp.einsum('bhgqs,bshd->bqhgd', p, vb) / den_t[..., None]
        lse = jnp.transpose(m + jnp.log(den), (0, 3, 1, 2))
        return o, lse

    o, lse = lax.map(one_block, jnp.arange(nb))
    o = jnp.moveaxis(o, 0, 1).reshape(b, n_pad, kvh, g, d)[:, :n]
    lse = jnp.moveaxis(lse, 0, 1).reshape(b, n_pad, kvh, g)[:, :n]
    return o, lse


def blocked_dense_attention(q, k, v, block):
    b, s, h, dk = q.shape
    dv = v.shape[-1]
    nb = s // block
    scale = dk ** -0.5
    kf = k.astype(jnp.float32)
    vf = v.astype(jnp.float32)
    qb = jnp.moveaxis(q.reshape(b, nb, block, h, dk), 1, 0)

    def one_block(qblk):
        sc = jnp.einsum('bqhd,bkhd->bhqk', qblk.astype(jnp.float32), kf) * scale
        p = jax.nn.softmax(sc, axis=-1)
        return jnp.einsum('bhqk,bkhd->bqhd', p, vf)

    o = lax.map(one_block, qb)
    return jnp.moveaxis(o, 0, 1).reshape(b, s, h, dv)


def mla_mixer(x, wq_a, q_a_norm, wq_b, wkv_a, kv_a_norm, wkv_b, q_norm, k_norm, wo):
    b, s, _ = x.shape
    cq = rms_norm(x @ wq_a, q_a_norm)
    q = (cq @ wq_b).reshape(b, s, MLA_HEADS, MLA_QK)
    kv_a = x @ wkv_a
    ckv = rms_norm(kv_a[..., :MLA_KV_LORA], kv_a_norm)
    k_rope = jnp.broadcast_to(kv_a[..., None, MLA_KV_LORA:], (b, s, MLA_HEADS, MLA_ROPE))
    kv = (ckv @ wkv_b).reshape(b, s, MLA_HEADS, MLA_NOPE + MLA_V)
    k = jnp.concatenate([kv[..., :MLA_NOPE], k_rope], axis=-1)
    v = kv[..., MLA_NOPE:]
    q = rms_norm(q, q_norm)
    k = rms_norm(k, k_norm)
    cos, sin = rope_tables(s, MLA_ROPE)
    q = jnp.concatenate([q[..., :MLA_NOPE], apply_rope(q[..., MLA_NOPE:], cos, sin)], axis=-1)
    k = jnp.concatenate([k[..., :MLA_NOPE], apply_rope(k[..., MLA_NOPE:], cos, sin)], axis=-1)
    o = blocked_dense_attention(q, k, v, MLA_Q_BLOCK)
    return o.reshape(b, s, MLA_HEADS * MLA_V).astype(x.dtype) @ wo


def swa_mixer(x, wqkv, q_norm, k_norm, sink, wo):
    b, s, _ = x.shape
    qkv = x @ wqkv
    nq = SWA_Q_HEADS * SWA_HEAD_DIM
    nk = SWA_KV_HEADS * SWA_HEAD_DIM
    q = qkv[..., :nq].reshape(b, s, SWA_Q_HEADS, SWA_HEAD_DIM)
    k = qkv[..., nq:nq + nk].reshape(b, s, SWA_KV_HEADS, SWA_HEAD_DIM)
    v = qkv[..., nq + nk:].reshape(b, s, SWA_KV_HEADS, SWA_HEAD_DIM)
    cos, sin = rope_tables(s, SWA_HEAD_DIM)
    q = apply_rope(rms_norm(q, q_norm), cos, sin)
    k = apply_rope(rms_norm(k, k_norm), cos, sin)
    q = q.reshape(b, s, SWA_KV_HEADS, SWA_GROUP, SWA_HEAD_DIM)
    o, _ = banded_attention(q, k, v, SWA_HALF_WINDOW, SWA_BLOCK, sink.reshape(SWA_KV_HEADS, SWA_GROUP))
    return o.reshape(b, s, SWA_Q_HEADS * SWA_HEAD_DIM).astype(x.dtype) @ wo


def dilated_mixer(x, wqkv, q_norm, k_norm, wo):
    b, s, _ = x.shape
    hg, d, ng = DIL_HEADS_PER_GROUP, DIL_HEAD_DIM, DIL_GROUPS
    qkv = (x @ wqkv).reshape(b, s, 3, ng * hg, d)
    cos, sin = rope_tables(s, d)
    q = apply_rope(rms_norm(qkv[:, :, 0], q_norm), cos, sin)
    k = apply_rope(rms_norm(qkv[:, :, 1], k_norm), cos, sin)
    v = qkv[:, :, 2]
    outs, lses = [], []
    for gi, (window, dil) in enumerate(DIL_CONFIGS):
        sl = slice(gi * hg, (gi + 1) * hg)
        half = window // (2 * dil)
        length = s // dil

        def to_strided(t):
            return jnp.transpose(t.reshape(b, length, dil, hg, d), (0, 2, 1, 3, 4)).reshape(b * dil, length, hg, d)

        qg = to_strided(q[:, :, sl])[:, :, :, None, :]
        o, lse = banded_attention(qg, to_strided(k[:, :, sl]), to_strided(v[:, :, sl]), half, DIL_BLOCK)
        o = jnp.transpose(o[:, :, :, 0].reshape(b, dil, length, hg, d), (0, 2, 1, 3, 4)).reshape(b, s, hg, d)
        lse = jnp.transpose(lse[..., 0].reshape(b, dil, length, hg), (0, 2, 1, 3)).reshape(b, s, hg)
        outs.append(o)
        lses.append(lse)
    o = jnp.stack(outs, axis=0)
    w = jax.nn.softmax(jnp.stack(lses, axis=0), axis=0)
    o = jnp.sum(w[..., None] * o, axis=0)
    return o.reshape(b, s, hg * d).astype(x.dtype) @ wo


def swiglu(x, w_gate, w_up, w_down):
    return (jax.nn.silu(x @ w_gate) * (x @ w_up)) @ w_down


def encoder_trunk(x, attn_norm, ffn_norm, w_gate, w_up, w_down,
                  mla_wq_a, mla_q_a_norm, mla_wq_b, mla_wkv_a, mla_kv_a_norm, mla_wkv_b,
                  mla_q_norm, mla_k_norm, mla_wo,
                  swa_wqkv, swa_q_norm, swa_k_norm, swa_sink, swa_wo,
                  dil_wqkv, dil_q_norm, dil_k_norm, dil_wo):
    for i in range(DEPTH):
        kind = i % N_MIXERS
        j = i // N_MIXERS
        h = rms_norm(x, attn_norm[i])
        if kind == 0:
            mix = mla_mixer(h, mla_wq_a[j], mla_q_a_norm[j], mla_wq_b[j], mla_wkv_a[j], mla_kv_a_norm[j],
                            mla_wkv_b[j], mla_q_norm[j], mla_k_norm[j], mla_wo[j])
        elif kind == 1:
            mix = swa_mixer(h, swa_wqkv[j], swa_q_norm[j], swa_k_norm[j], swa_sink[j], swa_wo[j])
        else:
            mix = dilated_mixer(h, dil_wqkv[j], dil_q_norm[j], dil_k_norm[j], dil_wo[j])
        x = x + mix
        x = x + swiglu(rms_norm(x, ffn_norm[i]), w_gate[i], w_up[i], w_down[i])
    return x


def setup_inputs(seed: int = 0) -> dict:
    key = jax.random.key(seed)
    ks = jax.random.split(key, 32)
    f32 = jnp.float32

    def nrm(k, shape, scale):
        return jax.random.normal(k, shape, f32) * scale

    def gain(k, shape):
        return 1.0 + 0.02 * jax.random.normal(k, shape, f32)

    na, nb_, nc = N_LAYERS_A, N_LAYERS_B, N_LAYERS_C
    return {
        'x_prompt': nrm(ks[0], (BATCH, SEQ, D_MODEL), 1.0),
        'x_sample': nrm(ks[1], (DEC_BATCH, DEC_SEQ, D_MODEL), 1.0),
        'attn_norm': gain(ks[2], (DEPTH, D_MODEL)),
        'ffn_norm': gain(ks[3], (DEPTH, D_MODEL)),
        'w_gate': nrm(ks[4], (DEPTH, D_MODEL, D_FF), D_MODEL ** -0.5),
        'w_up': nrm(ks[5], (DEPTH, D_MODEL, D_FF), D_MODEL ** -0.5),
        'w_down': nrm(ks[6], (DEPTH, D_FF, D_MODEL), D_FF ** -0.5),
        'mla_wq_a': nrm(ks[7], (na, D_MODEL, MLA_Q_LORA), D_MODEL ** -0.5),
        'mla_q_a_norm': gain(ks[8], (na, MLA_Q_LORA)),
        'mla_wq_b': nrm(ks[9], (na, MLA_Q_LORA, MLA_HEADS * MLA_QK), MLA_Q_LORA ** -0.5),
        'mla_wkv_a': nrm(ks[10], (na, D_MODEL, MLA_KV_LORA + MLA_ROPE), D_MODEL ** -0.5),
        'mla_kv_a_norm': gain(ks[11], (na, MLA_KV_LORA)),
        'mla_wkv_b': nrm(ks[12], (na, MLA_KV_LORA, MLA_HEADS * (MLA_NOPE + MLA_V)), MLA_KV_LORA ** -0.5),
        'mla_q_norm': gain(ks[13], (na, MLA_QK)),
        'mla_k_norm': gain(ks[14], (na, MLA_QK)),
        'mla_wo': nrm(ks[15], (na, MLA_HEADS * MLA_V, D_MODEL), (MLA_HEADS * MLA_V) ** -0.5),
        'swa_wqkv': nrm(ks[16], (nb_, D_MODEL, (SWA_Q_HEADS + 2 * SWA_KV_HEADS) * SWA_HEAD_DIM), D_MODEL ** -0.5),
        'swa_q_norm': gain(ks[17], (nb_, SWA_HEAD_DIM)),
        'swa_k_norm': gain(ks[18], (nb_, SWA_HEAD_DIM)),
        'swa_sink': nrm(ks[19], (nb_, SWA_Q_HEADS), 0.5),
        'swa_wo': nrm(ks[20], (nb_, SWA_Q_HEADS * SWA_HEAD_DIM, D_MODEL), (SWA_Q_HEADS * SWA_HEAD_DIM) ** -0.5),
        'dil_wqkv': nrm(ks[21], (nc, D_MODEL, 3 * DIL_GROUPS * DIL_HEADS_PER_GROUP * DIL_HEAD_DIM), D_MODEL ** -0.5),
        'dil_q_norm': gain(ks[22], (nc, DIL_HEAD_DIM)),
        'dil_k_norm': gain(ks[23], (nc, DIL_HEAD_DIM)),
        'dil_wo': nrm(ks[24], (nc, DIL_HEADS_PER_GROUP * DIL_HEAD_DIM, D_MODEL), (DIL_HEADS_PER_GROUP * DIL_HEAD_DIM) ** -0.5),
    }


def reference(x_prompt, x_sample, attn_norm, ffn_norm, w_gate, w_up, w_down,
              mla_wq_a, mla_q_a_norm, mla_wq_b, mla_wkv_a, mla_kv_a_norm, mla_wkv_b,
              mla_q_norm, mla_k_norm, mla_wo,
              swa_wqkv, swa_q_norm, swa_k_norm, swa_sink, swa_wo,
              dil_wqkv, dil_q_norm, dil_k_norm, dil_wo):
    weights = (attn_norm, ffn_norm, w_gate, w_up, w_down,
               mla_wq_a, mla_q_a_norm, mla_wq_b, mla_wkv_a, mla_kv_a_norm, mla_wkv_b,
               mla_q_norm, mla_k_norm, mla_wo,
               swa_wqkv, swa_q_norm, swa_k_norm, swa_sink, swa_wo,
               dil_wqkv, dil_q_norm, dil_k_norm, dil_wo)
    y_prompt = encoder_trunk(x_prompt, *weights)
    y_sample = encoder_trunk(x_sample, *weights)
    return (y_prompt, y_sample)
```

```python
import functools

import jax
import jax.numpy as jnp
from jax import lax
from jax.experimental import pallas as pl
from jax.experimental.pallas import tpu as pltpu

F32 = jnp.float32
BF16 = jnp.bfloat16

D_MODEL = 1024
DEPTH = 4
N_MIXERS = 3
ROPE_THETA = 10000.0
NORM_EPS = 1e-6
NEG_BIG = -1e30

MLA_HEADS = 8
MLA_Q_LORA = 384
MLA_KV_LORA = 256
MLA_NOPE = 128
MLA_ROPE = 64
MLA_V = 128
MLA_QK = MLA_NOPE + MLA_ROPE
MLA_QK_PAD = 256

SWA_Q_HEADS = 16
SWA_KV_HEADS = 4
SWA_HEAD_DIM = 64
SWA_HALF_WINDOW = 128

DIL_CONFIGS = ((128, 1), (512, 4), (2048, 16))
DIL_GROUPS = len(DIL_CONFIGS)
DIL_HEADS_PER_GROUP = 8
DIL_HEAD_DIM = 64
DIL_GROUP_WIDTH = DIL_HEADS_PER_GROUP * DIL_HEAD_DIM

D_FF = 2816

LANES = 128
NORM_CHUNK = 256
VMEM_LIMIT = 56 * 1024 * 1024

TM_PROJ = 256
TM_FFN = 256
TQ_MLA = 512
TQ_BAND = 256


def _rms(x, g):
    return x * lax.rsqrt(jnp.mean(x * x, axis=-1, keepdims=True) + NORM_EPS) * g


def _rope(v, c, s1, s2):
    return v * c + pltpu.roll(v, LANES - 32, 1) * s1 + pltpu.roll(v, 32, 1) * s2


def _resident(shape):
    return pl.BlockSpec(shape, lambda *_: (0,) * len(shape), pipeline_mode=pl.Buffered(1))


def _params(n_axes):
    return pltpu.CompilerParams(dimension_semantics=("parallel",) * n_axes,
                                vmem_limit_bytes=VMEM_LIMIT)


def _mla_proj_kernel(x_ref, an_ref, wqa_ref, qan_ref, wqb_ref, wkva_ref, kvan_ref, wkvb_ref,
                     gq_ref, gk_ref, c_ref, s1_ref, s2_ref, q_out, k_out, v_out):
    h = _rms(x_ref[...], an_ref[...]).astype(BF16)
    cq = jnp.dot(h, wqa_ref[...], preferred_element_type=F32)
    cq = _rms(cq, qan_ref[...]).astype(BF16)
    qf = jnp.dot(cq, wqb_ref[...], preferred_element_type=F32)
    kva = jnp.dot(h, wkva_ref[...], preferred_element_type=F32)
    ckv = _rms(kva[:, :MLA_KV_LORA], kvan_ref[...]).astype(BF16)
    kv = jnp.dot(ckv, wkvb_ref[...], preferred_element_type=F32)
    c, s1, s2 = c_ref[...], s1_ref[...], s2_ref[...]
    gq, gk = gq_ref[...], gk_ref[...]
    scale = MLA_QK ** -0.5
    kr = kva[:, MLA_KV_LORA:]
    kr_ss = jnp.sum(kr * kr, axis=-1, keepdims=True)
    kr_roped = _rope(kr * gk[:, LANES:], c, s1, s2)
    for hh in range(MLA_HEADS):
        lo = hh * MLA_QK_PAD
        qh = qf[:, lo:lo + MLA_QK_PAD]
        rq = lax.rsqrt(jnp.sum(qh * qh, axis=-1, keepdims=True) * (1.0 / MLA_QK) + NORM_EPS) * scale
        qn = qh * rq * gq
        q_out[:, lo:lo + LANES] = qn[:, :LANES].astype(BF16)
        q_out[:, lo + LANES:lo + MLA_QK_PAD] = _rope(qn[:, LANES:], c, s1, s2).astype(BF16)
        kn = kv[:, lo:lo + LANES]
        rk = lax.rsqrt((jnp.sum(kn * kn, axis=-1, keepdims=True) + kr_ss) * (1.0 / MLA_QK) + NORM_EPS)
        k_out[:, lo:lo + LANES] = (kn * rk * gk[:, :LANES]).astype(BF16)
        k_out[:, lo + LANES:lo + MLA_QK_PAD] = (kr_roped * rk).astype(BF16)
        v_out[:, hh * MLA_V:(hh + 1) * MLA_V] = kv[:, lo + LANES:lo + MLA_QK_PAD].astype(BF16)


def _mla_proj(x, seq, an, wqa, qan, wqb, wkva, kvan, wkvb, gq, gk, tabs):
    m = x.shape[0]
    tm = TM_PROJ
    nblk = seq // tm
    row = lambda i: (i, 0)
    pos = lambda i: (i % nblk, 0)
    hq = MLA_HEADS * MLA_QK_PAD
    in_specs = [pl.BlockSpec((tm, D_MODEL), row), _resident(an.shape), _resident(wqa.shape),
                _resident(qan.shape), _resident(wqb.shape), _resident(wkva.shape),
                _resident(kvan.shape), _resident(wkvb.shape), _resident(gq.shape), _resident(gk.shape)]
    in_specs += [pl.BlockSpec((tm, LANES), pos)] * 3
    return pl.pallas_call(
        _mla_proj_kernel,
        out_shape=(jax.ShapeDtypeStruct((m, hq), BF16), jax.ShapeDtypeStruct((m, hq), BF16),
                   jax.ShapeDtypeStruct((m, MLA_HEADS * MLA_V), BF16)),
        grid=(m // tm,),
        in_specs=in_specs,
        out_specs=(pl.BlockSpec((tm, hq), row), pl.BlockSpec((tm, hq), row),
                   pl.BlockSpec((tm, MLA_HEADS * MLA_V), row)),
        compiler_params=_params(1),
        name="mla_proj",
    )(x, an, wqa, qan, wqb, wkva, kvan, wkvb, gq, gk, *tabs)


def _mla_attn_kernel(q_ref, k_ref, v_ref, o_ref, *, seq, tq):
    def body(i, carry):
        r0 = pl.multiple_of(i * tq, tq)
        q = q_ref[0, pl.ds(r0, tq), :]
        s = lax.dot_general(q, k_ref[0], (((1,), (1,)), ((), ())), preferred_element_type=F32)
        m = jnp.max(s, axis=-1, keepdims=True)
        p = jnp.exp(s - m)
        den = jnp.sum(p, axis=-1, keepdims=True)
        o = jnp.dot(p.astype(BF16), v_ref[0], preferred_element_type=F32)
        o_ref[0, pl.ds(r0, tq), :] = (o / den).astype(o_ref.dtype)
        return carry

    lax.fori_loop(0, seq // tq, body, 0)


def _mla_attn(q, k, v):
    b, seq, _ = q.shape
    return pl.pallas_call(
        functools.partial(_mla_attn_kernel, seq=seq, tq=TQ_MLA),
        out_shape=jax.ShapeDtypeStruct((b, seq, MLA_HEADS * MLA_V), BF16),
        grid=(b, MLA_HEADS),
        in_specs=[pl.BlockSpec((1, seq, MLA_QK_PAD), lambda bi, hi: (bi, 0, hi)),
                  pl.BlockSpec((1, seq, MLA_QK_PAD), lambda bi, hi: (bi, 0, hi)),
                  pl.BlockSpec((1, seq, MLA_V), lambda bi, hi: (bi, 0, hi))],
        out_specs=pl.BlockSpec((1, seq, MLA_V), lambda bi, hi: (bi, 0, hi)),
        compiler_params=_params(2),
        name="mla_attn",
    )(q, k, v)


def _qkv_proj_kernel(x_ref, an_ref, w_ref, gq_ref, gk_ref, gmat_ref, c_ref, s1_ref, s2_ref,
                     q_out, k_out, v_out, *, nq, nk):
    h = _rms(x_ref[...], an_ref[...]).astype(BF16)
    qkv = jnp.dot(h, w_ref[...], preferred_element_type=F32)
    gmat = gmat_ref[...]
    c, s1, s2 = c_ref[...], s1_ref[...], s2_ref[...]

    def norm_rope(t, g):
        sq = t * t
        hi = sq.astype(BF16)
        lo = (sq - hi.astype(F32)).astype(BF16)
        ms = (jnp.dot(hi, gmat, preferred_element_type=F32)
              + jnp.dot(lo, gmat, preferred_element_type=F32))
        return t * lax.rsqrt(ms + NORM_EPS) * g

    def emit(out_ref, col0, width, g):
        for j in range(width // NORM_CHUNK):
            tn = norm_rope(qkv[:, col0 + j * NORM_CHUNK:col0 + (j + 1) * NORM_CHUNK], g)
            for half in range(NORM_CHUNK // LANES):
                lo = j * NORM_CHUNK + half * LANES
                out_ref[:, lo:lo + LANES] = _rope(tn[:, half * LANES:(half + 1) * LANES], c, s1, s2).astype(BF16)

    emit(q_out, 0, nq, gq_ref[...])
    emit(k_out, nq, nk, gk_ref[...])
    v_out[...] = qkv[:, nq + nk:].astype(BF16)


def _qkv_proj(x, seq, an, w, gq, gk, gmat, tabs, nq, nk):
    m = x.shape[0]
    tm = TM_PROJ
    nblk = seq // tm
    nv = w.shape[1] - nq - nk
    row = lambda i: (i, 0)
    pos = lambda i: (i % nblk, 0)
    in_specs = [pl.BlockSpec((tm, D_MODEL), row), _resident(an.shape), _resident(w.shape),
                _resident(gq.shape), _resident(gk.shape), _resident(gmat.shape)]
    in_specs += [pl.BlockSpec((tm, LANES), pos)] * 3
    return pl.pallas_call(
        functools.partial(_qkv_proj_kernel, nq=nq, nk=nk),
        out_shape=(jax.ShapeDtypeStruct((m, nq), BF16), jax.ShapeDtypeStruct((m, nk), BF16),
                   jax.ShapeDtypeStruct((m, nv), BF16)),
        grid=(m // tm,),
        in_specs=in_specs,
        out_specs=(pl.BlockSpec((tm, nq), row), pl.BlockSpec((tm, nk), row), pl.BlockSpec((tm, nv), row)),
        compiler_params=_params(1),
        name="qkv_proj",
    )(x, an, w, gq, gk, gmat, *tabs)


def _banded_kernel(*refs, length, tq, win, half, q_heads, group, d, has_sink, want_lse):
    refs = list(refs)
    q_ref, k_ref, v_ref = refs[:3]
    sink_ref = refs[3] if has_sink else None
    o_ref = refs[3 + has_sink]
    lse_ref = refs[4 + has_sink] if want_lse else None
    kv_heads = q_heads // group

    def tile(q0, start):
        dd = (lax.broadcasted_iota(jnp.int32, (tq, win), 0)
              - lax.broadcasted_iota(jnp.int32, (tq, win), 1))
        mask = jnp.abs(dd + (q0 - start)) <= half
        for g in range(kv_heads):
            kg = k_ref[0, pl.ds(start, win), g * d:(g + 1) * d]
            vg = v_ref[0, pl.ds(start, win), g * d:(g + 1) * d]
            for a in range(g * group, (g + 1) * group):
                qa = q_ref[0, pl.ds(q0, tq), a * d:(a + 1) * d]
                s = lax.dot_general(qa, kg, (((1,), (1,)), ((), ())), preferred_element_type=F32)
                s = jnp.where(mask, s, NEG_BIG)
                m = jnp.max(s, axis=-1, keepdims=True)
                if has_sink:
                    sk = sink_ref[:, a:a + 1]
                    m = jnp.maximum(m, sk)
                p = jnp.exp(s - m)
                den = jnp.sum(p, axis=-1, keepdims=True)
                if has_sink:
                    den = den + jnp.exp(sk - m)
                o = jnp.dot(p.astype(BF16), vg, preferred_element_type=F32) / den
                o_ref[0, pl.ds(q0, tq), a * d:(a + 1) * d] = o.astype(o_ref.dtype)
                if want_lse:
                    lse_ref[0, pl.ds(q0, tq), a * d:(a + 1) * d] = jnp.broadcast_to(m + jnp.log(den), (tq, d))

    n_tiles = length // tq
    if n_tiles == 1:
        tile(0, 0)
    else:
        def body(i, carry):
            q0 = pl.multiple_of(i * tq, tq)
            start = pl.multiple_of(jnp.clip(q0 - half, 0, length - win), half)
            tile(q0, start)
            return carry

        lax.fori_loop(0, n_tiles, body, 0)


def _banded_attn(q, k, v, sink, *, n_res, col_mult, col_off, q_width, kv_width, half, group, want_lse):
    b, length, _ = q.shape
    tq = min(TQ_BAND, length)
    win = min(length, tq + 2 * half)
    q_heads = q_width // SWA_HEAD_DIM
    has_sink = sink is not None
    col = lambda bi, ri: (bi, 0, ri * col_mult + col_off)
    out_col = lambda bi, ri: (bi, 0, ri)
    in_specs = [pl.BlockSpec((1, length, q_width), col), pl.BlockSpec((1, length, kv_width), col),
                pl.BlockSpec((1, length, kv_width), col)]
    args = [q, k, v]
    if has_sink:
        in_specs.append(_resident(sink.shape))
        args.append(sink)
    out_shape = [jax.ShapeDtypeStruct((b, length, n_res * q_width), BF16)]
    out_specs = [pl.BlockSpec((1, length, q_width), out_col)]
    if want_lse:
        out_shape.append(jax.ShapeDtypeStruct((b, length, n_res * q_width), F32))
        out_specs.append(pl.BlockSpec((1, length, q_width), out_col))
    return pl.pallas_call(
        functools.partial(_banded_kernel, length=length, tq=tq, win=win, half=half, q_heads=q_heads,
                          group=group, d=SWA_HEAD_DIM, has_sink=has_sink, want_lse=want_lse),
        out_shape=tuple(out_shape),
        grid=(b, n_res),
        in_specs=in_specs,
        out_specs=tuple(out_specs),
        compiler_params=_params(2),
        name="banded_attn",
    )(*args)


def _out_ffn_kernel(*refs, n_mix):
    refs = list(refs)
    x_ref = refs[0]
    mix = refs[1:-6]
    wo_ref, fn_ref, wg_ref, wu_ref, wd_ref, y_ref = refs[-6:]
    if n_mix == 1:
        o = mix[0][...]
    else:
        outs = [r[...].astype(F32) for r in mix[0::2]]
        lses = [r[...] for r in mix[1::2]]
        mx = functools.reduce(jnp.maximum, lses)
        es = [jnp.exp(l - mx) for l in lses]
        den = functools.reduce(jnp.add, es)
        o = (functools.reduce(jnp.add, [e * t for e, t in zip(es, outs)]) / den).astype(BF16)
    x1 = x_ref[...] + jnp.dot(o, wo_ref[...], preferred_element_type=F32)
    h = _rms(x1, fn_ref[...]).astype(BF16)
    gate = jnp.dot(h, wg_ref[...], preferred_element_type=F32)
    up = jnp.dot(h, wu_ref[...], preferred_element_type=F32)
    act = (gate * jax.nn.sigmoid(gate) * up).astype(BF16)
    y_ref[...] = x1 + jnp.dot(act, wd_ref[...], preferred_element_type=F32)


def _out_ffn(x, mix, wo, fn, wg, wu, wd):
    m = x.shape[0]
    tm = TM_FFN
    row = lambda i: (i, 0)
    n_mix = (len(mix) + 1) // 2
    in_specs = [pl.BlockSpec((tm, D_MODEL), row)]
    in_specs += [pl.BlockSpec((tm, t.shape[1]), row) for t in mix]
    in_specs += [_resident(wo.shape), _resident(fn.shape), _resident(wg.shape), _resident(wu.shape),
                 _resident(wd.shape)]
    return pl.pallas_call(
        functools.partial(_out_ffn_kernel, n_mix=n_mix),
        out_shape=jax.ShapeDtypeStruct((m, D_MODEL), F32),
        grid=(m // tm,),
        in_specs=in_specs,
        out_specs=pl.BlockSpec((tm, D_MODEL), row),
        compiler_params=_params(1),
        name="out_ffn",
    )(x, *mix, wo, fn, wg, wu, wd)


def _rope_tables(seq, dim, lanes):
    inv = 1.0 / (ROPE_THETA ** (jnp.arange(0, dim, 2, dtype=F32) / dim))
    ang = jnp.arange(seq, dtype=F32)[:, None] * inv[None, :]
    cos, sin = jnp.cos(ang), jnp.sin(ang)
    zero = jnp.zeros_like(sin)
    tabs = (jnp.concatenate([cos, cos], -1), jnp.concatenate([-sin, zero], -1),
            jnp.concatenate([zero, sin], -1))
    return tuple(jnp.tile(t, (1, lanes // dim)) for t in tabs)


def _row(v):
    return v.reshape(1, -1).astype(F32)


def _head_gain(g, reps, scale=1.0):
    return jnp.tile(g.astype(F32) * scale, reps).reshape(1, -1)


def _trunk(x, attn_norm, ffn_norm, w_gate, w_up, w_down,
           mla_wq_a, mla_q_a_norm, mla_wq_b, mla_wkv_a, mla_kv_a_norm, mla_wkv_b,
           mla_q_norm, mla_k_norm, mla_wo,
           swa_wqkv, swa_q_norm, swa_k_norm, swa_sink, swa_wo,
           dil_wqkv, dil_q_norm, dil_k_norm, dil_wo):
    b, seq, _ = x.shape
    m = b * seq
    xf = x.reshape(m, D_MODEL)
    tabs64 = _rope_tables(seq, SWA_HEAD_DIM, LANES)
    cos, msin, psin = _rope_tables(seq, MLA_ROPE, MLA_ROPE)
    pad = ((0, 0), (0, LANES - MLA_ROPE))
    tabs_mla = (jnp.pad(cos, pad), jnp.pad(msin, pad), jnp.pad(psin, pad))
    blk = jnp.arange(NORM_CHUNK) // SWA_HEAD_DIM
    gmat = ((blk[:, None] == blk[None, :]).astype(F32) / SWA_HEAD_DIM).astype(BF16)

    for i in range(DEPTH):
        kind, j = i % N_MIXERS, i // N_MIXERS
        an = _row(attn_norm[i])
        if kind == 0:
            wqb = mla_wq_b[j].reshape(MLA_Q_LORA, MLA_HEADS, MLA_QK)
            wqb = jnp.pad(wqb, ((0, 0), (0, 0), (0, MLA_QK_PAD - MLA_QK))).reshape(MLA_Q_LORA, -1)
            wkva = jnp.pad(mla_wkv_a[j], ((0, 0), (0, LANES - MLA_ROPE)))
            gpad = (0, MLA_QK_PAD - MLA_QK)
            q, k, v = _mla_proj(
                xf, seq, an, mla_wq_a[j].astype(BF16), _row(mla_q_a_norm[j]), wqb.astype(BF16),
                wkva.astype(BF16), _row(mla_kv_a_norm[j]), mla_wkv_b[j].astype(BF16),
                _row(jnp.pad(mla_q_norm[j], gpad)), _row(jnp.pad(mla_k_norm[j], gpad)), tabs_mla)
            o = _mla_attn(q.reshape(b, seq, -1), k.reshape(b, seq, -1), v.reshape(b, seq, -1))
            mix, wo = [o.reshape(m, -1)], mla_wo[j]
        elif kind == 1:
            nq, nk = SWA_Q_HEADS * SWA_HEAD_DIM, SWA_KV_HEADS * SWA_HEAD_DIM
            reps = NORM_CHUNK // SWA_HEAD_DIM
            q, k, v = _qkv_proj(xf, seq, an, swa_wqkv[j].astype(BF16),
                                _head_gain(swa_q_norm[j], reps, SWA_HEAD_DIM ** -0.5),
                                _head_gain(swa_k_norm[j], reps), gmat, tabs64, nq, nk)
            (o,) = _banded_attn(q.reshape(b, seq, nq), k.reshape(b, seq, nk), v.reshape(b, seq, nk),
                                _row(swa_sink[j]), n_res=1, col_mult=0, col_off=0, q_width=nq, kv_width=nk,
                                half=SWA_HALF_WINDOW, group=SWA_Q_HEADS // SWA_KV_HEADS, want_lse=False)
            mix, wo = [o.reshape(m, nq)], swa_wo[j]
        else:
            nq = DIL_GROUPS * DIL_GROUP_WIDTH
            reps = NORM_CHUNK // DIL_HEAD_DIM
            q, k, v = _qkv_proj(xf, seq, an, dil_wqkv[j].astype(BF16),
                                _head_gain(dil_q_norm[j], reps, DIL_HEAD_DIM ** -0.5),
                                _head_gain(dil_k_norm[j], reps), gmat, tabs64, nq, nq)
            mix = []
            for gi, (window, dil) in enumerate(DIL_CONFIGS):
                length = seq // dil
                view = lambda t: t.reshape(b, length, dil * nq)
                o, lse = _banded_attn(view(q), view(k), view(v), None, n_res=dil, col_mult=DIL_GROUPS,
                                      col_off=gi, q_width=DIL_GROUP_WIDTH, kv_width=DIL_GROUP_WIDTH,
                                      half=window // (2 * dil), group=1, want_lse=True)
                mix += [o.reshape(m, DIL_GROUP_WIDTH), lse.reshape(m, DIL_GROUP_WIDTH)]
            wo = dil_wo[j]
        xf = _out_ffn(xf, mix, wo.astype(BF16), _row(ffn_norm[i]), w_gate[i].astype(BF16),
                      w_up[i].astype(BF16), w_down[i].astype(BF16))
    return xf.reshape(b, seq, D_MODEL)


def kernel(x_prompt, x_sample, attn_norm, ffn_norm, w_gate, w_up, w_down, mla_wq_a, mla_q_a_norm, mla_wq_b, mla_wkv_a, mla_kv_a_norm, mla_wkv_b, mla_q_norm, mla_k_norm, mla_wo, swa_wqkv, swa_q_norm, swa_k_norm, swa_sink, swa_wo, dil_wqkv, dil_q_norm, dil_k_norm, dil_wo):
    weights = (attn_norm, ffn_norm, w_gate, w_up, w_down,
               mla_wq_a, mla_q_a_norm, mla_wq_b, mla_wkv_a, mla_kv_a_norm, mla_wkv_b,
               mla_q_norm, mla_k_norm, mla_wo,
               swa_wqkv, swa_q_norm, swa_k_norm, swa_sink, swa_wo,
               dil_wqkv, dil_q_norm, dil_k_norm, dil_wo)
    return (_trunk(x_prompt, *weights), _trunk(x_sample, *weights))
```

```python
import functools
import math

import jax
import jax.numpy as jnp
from jax import lax
from jax.experimental import pallas as pl
from jax.experimental.pallas import tpu as pltpu

F32 = jnp.float32
BF16 = jnp.bfloat16

D_MODEL = 1024
DEPTH = 4
N_MIXERS = 3
ROPE_THETA = 10000.0
NORM_EPS = 1e-6
NEG_BIG = -1e30
LOG2E = math.log2(math.e)

MLA_HEADS = 8
MLA_Q_LORA = 384
MLA_KV_LORA = 256
MLA_NOPE = 128
MLA_ROPE = 64
MLA_V = 128
MLA_QK = MLA_NOPE + MLA_ROPE
MLA_QK_PAD = 256

SWA_Q_HEADS = 16
SWA_KV_HEADS = 4
SWA_HEAD_DIM = 64
SWA_HALF_WINDOW = 128

DIL_CONFIGS = ((128, 1), (512, 4), (2048, 16))
DIL_GROUPS = len(DIL_CONFIGS)
DIL_HEADS_PER_GROUP = 8
DIL_HEAD_DIM = 64
DIL_GROUP_WIDTH = DIL_HEADS_PER_GROUP * DIL_HEAD_DIM

D_FF = 2816

LANES = 128
HEAD_DIM = 64
NORM_CHUNK = 256
VMEM_LIMIT = 56 * 1024 * 1024

TM_PROJ = 512
SUB_PROJ = 256
TM_FFN = 256
TQ_MLA = 256
UNROLL_MLA = 4
TQ_SWA = 256
TQ_DIL = 128


def _rms(x, g):
    return x * lax.rsqrt(jnp.mean(x * x, axis=-1, keepdims=True) + NORM_EPS) * g


def _rope(v, c, s1, s2):
    return v * c + pltpu.roll(v, LANES - 32, 1) * s1 + pltpu.roll(v, 32, 1) * s2


def _resident(shape):
    return pl.BlockSpec(shape, lambda *_: (0,) * len(shape), pipeline_mode=pl.Buffered(1))


def _params(n_axes):
    return pltpu.CompilerParams(dimension_semantics=("parallel",) * n_axes,
                                vmem_limit_bytes=VMEM_LIMIT)


def _mla_proj_kernel(x_ref, an_ref, wqa_ref, qan_ref, wqb_ref, wkva_ref, kvan_ref, wkvb_ref,
                     gq_ref, gk_ref, c_ref, s1_ref, s2_ref, q_out, k_out, v_out):
    gq, gk = gq_ref[...], gk_ref[...]
    scale = MLA_QK ** -0.5 * LOG2E
    for sub in range(TM_PROJ // SUB_PROJ):
        rows = slice(sub * SUB_PROJ, (sub + 1) * SUB_PROJ)
        h = _rms(x_ref[rows, :], an_ref[...]).astype(BF16)
        cq = jnp.dot(h, wqa_ref[...], preferred_element_type=F32)
        cq = _rms(cq, qan_ref[...]).astype(BF16)
        qf = jnp.dot(cq, wqb_ref[...], preferred_element_type=F32)
        kva = jnp.dot(h, wkva_ref[...], preferred_element_type=F32)
        ckv = _rms(kva[:, :MLA_KV_LORA], kvan_ref[...]).astype(BF16)
        kv = jnp.dot(ckv, wkvb_ref[...], preferred_element_type=F32)
        c, s1, s2 = c_ref[rows, :], s1_ref[rows, :], s2_ref[rows, :]
        kr = kva[:, MLA_KV_LORA:]
        kr_ss = jnp.sum(kr * kr, axis=-1, keepdims=True)
        kr_roped = _rope(kr * gk[:, LANES:], c, s1, s2)
        for hh in range(MLA_HEADS):
            lo = hh * MLA_QK_PAD
            qh = qf[:, lo:lo + MLA_QK_PAD]
            rq = lax.rsqrt(jnp.sum(qh * qh, axis=-1, keepdims=True) * (1.0 / MLA_QK) + NORM_EPS) * scale
            qn = qh * rq * gq
            q_out[rows, lo:lo + LANES] = qn[:, :LANES].astype(BF16)
            q_out[rows, lo + LANES:lo + MLA_QK_PAD] = _rope(qn[:, LANES:], c, s1, s2).astype(BF16)
            kn = kv[:, lo:lo + LANES]
            rk = lax.rsqrt((jnp.sum(kn * kn, axis=-1, keepdims=True) + kr_ss) * (1.0 / MLA_QK) + NORM_EPS)
            k_out[rows, lo:lo + LANES] = (kn * rk * gk[:, :LANES]).astype(BF16)
            k_out[rows, lo + LANES:lo + MLA_QK_PAD] = (kr_roped * rk).astype(BF16)
            v_out[rows, hh * MLA_V:(hh + 1) * MLA_V] = kv[:, lo + LANES:lo + MLA_QK_PAD].astype(BF16)


def _mla_proj(x, seq, an, wqa, qan, wqb, wkva, kvan, wkvb, gq, gk, tabs):
    m = x.shape[0]
    tm = TM_PROJ
    nblk = seq // tm
    row = lambda i: (i, 0)
    pos = lambda i: (i % nblk, 0)
    hq = MLA_HEADS * MLA_QK_PAD
    in_specs = [pl.BlockSpec((tm, D_MODEL), row), _resident(an.shape), _resident(wqa.shape),
                _resident(qan.shape), _resident(wqb.shape), _resident(wkva.shape),
                _resident(kvan.shape), _resident(wkvb.shape), _resident(gq.shape), _resident(gk.shape)]
    in_specs += [pl.BlockSpec((tm, LANES), pos)] * 3
    return pl.pallas_call(
        _mla_proj_kernel,
        out_shape=(jax.ShapeDtypeStruct((m, hq), BF16), jax.ShapeDtypeStruct((m, hq), BF16),
                   jax.ShapeDtypeStruct((m, MLA_HEADS * MLA_V), BF16)),
        grid=(m // tm,),
        in_specs=in_specs,
        out_specs=(pl.BlockSpec((tm, hq), row), pl.BlockSpec((tm, hq), row),
                   pl.BlockSpec((tm, MLA_HEADS * MLA_V), row)),
        compiler_params=_params(1),
        name="mla_proj",
    )(x, an, wqa, qan, wqb, wkva, kvan, wkvb, gq, gk, *tabs)


def _mla_attn_kernel(q_ref, k_ref, v_ref, o_ref, v1_scr, *, seq, tq):
    v1_scr[:, :MLA_V] = v_ref[0]
    v1_scr[:, MLA_V:] = jnp.ones((seq, MLA_V), BF16)

    def body(i, carry):
        r0 = pl.multiple_of(i * tq, tq)
        q = q_ref[0, pl.ds(r0, tq), :]
        s = lax.dot_general(q, k_ref[0], (((1,), (1,)), ((), ())), preferred_element_type=F32)
        m = jnp.max(s, axis=-1, keepdims=True)
        p = jnp.exp2(s - m).astype(BF16)
        o = jnp.dot(p, v1_scr[...], preferred_element_type=F32)
        o_ref[0, pl.ds(r0, tq), :] = (o[:, :MLA_V] / o[:, MLA_V:]).astype(o_ref.dtype)
        return carry

    lax.fori_loop(0, seq // tq, body, 0, unroll=UNROLL_MLA)


def _mla_attn(q, k, v):
    b, seq, _ = q.shape
    return pl.pallas_call(
        functools.partial(_mla_attn_kernel, seq=seq, tq=TQ_MLA),
        out_shape=jax.ShapeDtypeStruct((b, seq, MLA_HEADS * MLA_V), BF16),
        grid=(b, MLA_HEADS),
        in_specs=[pl.BlockSpec((1, seq, MLA_QK_PAD), lambda bi, hi: (bi, 0, hi)),
                  pl.BlockSpec((1, seq, MLA_QK_PAD), lambda bi, hi: (bi, 0, hi)),
                  pl.BlockSpec((1, seq, MLA_V), lambda bi, hi: (bi, 0, hi))],
        out_specs=pl.BlockSpec((1, seq, MLA_V), lambda bi, hi: (bi, 0, hi)),
        scratch_shapes=[pltpu.VMEM((seq, 2 * MLA_V), BF16)],
        compiler_params=_params(2),
        name="mla_attn",
    )(q, k, v)


def _normed_roped_chunks(qkv, col0, width, g, gmat, c, s1, s2):
    for j in range(width // NORM_CHUNK):
        t = qkv[:, col0 + j * NORM_CHUNK:col0 + (j + 1) * NORM_CHUNK]
        ms = jnp.dot((t * t).astype(BF16), gmat, preferred_element_type=F32)
        tn = t * lax.rsqrt(ms + NORM_EPS) * g
        for half in range(NORM_CHUNK // LANES):
            yield (j * NORM_CHUNK + half * LANES,
                   _rope(tn[:, half * LANES:(half + 1) * LANES], c, s1, s2))


def _swa_proj_kernel(x_ref, an_ref, w_ref, gq_ref, gk_ref, gmat_ref, c_ref, s1_ref, s2_ref,
                     q_out, k_out, v_out, *, nq, nk):
    gmat = gmat_ref[...]
    for sub in range(TM_PROJ // SUB_PROJ):
        rows = slice(sub * SUB_PROJ, (sub + 1) * SUB_PROJ)
        h = _rms(x_ref[rows, :], an_ref[...]).astype(BF16)
        qkv = jnp.dot(h, w_ref[...], preferred_element_type=F32)
        tabs = (c_ref[rows, :], s1_ref[rows, :], s2_ref[rows, :])
        for lo, val in _normed_roped_chunks(qkv, 0, nq, gq_ref[...], gmat, *tabs):
            q_out[rows, lo:lo + LANES] = val.astype(BF16)
        for lo, val in _normed_roped_chunks(qkv, nq, nk, gk_ref[...], gmat, *tabs):
            k_out[rows, lo:lo + LANES] = val.astype(BF16)
        v_out[rows, :] = qkv[:, nq + nk:].astype(BF16)


def _swa_proj(x, seq, an, w, gq, gk, gmat, tabs, nq, nk):
    m = x.shape[0]
    tm = TM_PROJ
    nblk = seq // tm
    nv = w.shape[1] - nq - nk
    row = lambda i: (i, 0)
    pos = lambda i: (i % nblk, 0)
    in_specs = [pl.BlockSpec((tm, D_MODEL), row), _resident(an.shape), _resident(w.shape),
                _resident(gq.shape), _resident(gk.shape), _resident(gmat.shape)]
    in_specs += [pl.BlockSpec((tm, LANES), pos)] * 3
    return pl.pallas_call(
        functools.partial(_swa_proj_kernel, nq=nq, nk=nk),
        out_shape=(jax.ShapeDtypeStruct((m, nq), BF16), jax.ShapeDtypeStruct((m, nk), BF16),
                   jax.ShapeDtypeStruct((m, nv), BF16)),
        grid=(m // tm,),
        in_specs=in_specs,
        out_specs=(pl.BlockSpec((tm, nq), row), pl.BlockSpec((tm, nk), row), pl.BlockSpec((tm, nv), row)),
        compiler_params=_params(1),
        name="swa_proj",
    )(x, an, w, gq, gk, gmat, *tabs)


def _dil_proj_kernel(x_ref, an_ref, w_ref, gq_ref, gk_ref, gmat_ref, c_ref, s1_ref, s2_ref, *refs):
    outs, stage = refs[:3 * DIL_GROUPS], refs[3 * DIL_GROUPS]
    gmat = gmat_ref[...]
    nq = DIL_GROUPS * DIL_GROUP_WIDTH
    slabs_per_group = DIL_GROUP_WIDTH // LANES
    for sub in range(TM_PROJ // SUB_PROJ):
        r0 = sub * SUB_PROJ
        rows = slice(r0, r0 + SUB_PROJ)
        h = _rms(x_ref[rows, :], an_ref[...]).astype(BF16)
        qkv = jnp.dot(h, w_ref[...], preferred_element_type=F32)
        tabs = (c_ref[rows, :], s1_ref[rows, :], s2_ref[rows, :])

        def emit(tensor, lo, val, sub=sub, r0=r0):
            gi, c = lo // DIL_GROUP_WIDTH, (lo % DIL_GROUP_WIDTH) // LANES
            dil = DIL_CONFIGS[gi][1]
            out = outs[3 * gi + tensor]
            cols = slice(c * LANES, (c + 1) * LANES)
            if dil == 1:
                out[0, 0, r0:r0 + SUB_PROJ, cols] = val.astype(BF16)
                return
            slot = ((sub * 3 + tensor) * (DIL_GROUPS - 1) + gi - 1) * slabs_per_group + c
            stage[slot] = val
            n = SUB_PROJ // dil
            for r in range(dil):
                out[0, r, r0 // dil:r0 // dil + n, cols] = stage[slot, pl.ds(r, n, stride=dil), :].astype(BF16)

        for lo, val in _normed_roped_chunks(qkv, 0, nq, gq_ref[...], gmat, *tabs):
            emit(0, lo, val)
        for lo, val in _normed_roped_chunks(qkv, nq, nq, gk_ref[...], gmat, *tabs):
            emit(1, lo, val)
        for lo in range(0, nq, LANES):
            emit(2, lo, qkv[:, 2 * nq + lo:2 * nq + lo + LANES])


def _dil_proj(x, b, seq, an, w, gq, gk, gmat, tabs):
    m = x.shape[0]
    tm = TM_PROJ
    nblk = seq // tm
    row = lambda i: (i, 0)
    pos = lambda i: (i % nblk, 0)
    in_specs = [pl.BlockSpec((tm, D_MODEL), row), _resident(an.shape), _resident(w.shape),
                _resident(gq.shape), _resident(gk.shape), _resident(gmat.shape)]
    in_specs += [pl.BlockSpec((tm, LANES), pos)] * 3
    out_shape, out_specs = [], []
    for _, dil in DIL_CONFIGS:
        for _ in range(3):
            out_shape.append(jax.ShapeDtypeStruct((b, dil, seq // dil, DIL_GROUP_WIDTH), BF16))
            out_specs.append(pl.BlockSpec((1, dil, tm // dil, DIL_GROUP_WIDTH),
                                          lambda i: (i // nblk, 0, i % nblk, 0)))
    n_slots = (TM_PROJ // SUB_PROJ) * 3 * (DIL_GROUPS - 1) * (DIL_GROUP_WIDTH // LANES)
    return pl.pallas_call(
        _dil_proj_kernel,
        out_shape=tuple(out_shape),
        grid=(m // tm,),
        in_specs=in_specs,
        out_specs=tuple(out_specs),
        scratch_shapes=[pltpu.VMEM((n_slots, SUB_PROJ, LANES), F32)],
        compiler_params=_params(1),
        name="dil_proj",
    )(x, an, w, gq, gk, gmat, *tabs)


def _swa_attn_kernel(q_ref, k_ref, v_ref, sink_ref, o_ref, *, length, tq, win, half, q_heads, group, d):
    kv_heads = q_heads // group

    def body(i, carry):
        q0 = pl.multiple_of(i * tq, tq)
        start = pl.multiple_of(jnp.clip(q0 - half, 0, length - win), half)
        dd = (lax.broadcasted_iota(jnp.int32, (tq, win), 0)
              - lax.broadcasted_iota(jnp.int32, (tq, win), 1))
        mask = jnp.abs(dd + (q0 - start)) <= half
        for g in range(kv_heads):
            kg = k_ref[0, pl.ds(start, win), g * d:(g + 1) * d]
            vg = v_ref[0, pl.ds(start, win), g * d:(g + 1) * d]
            for a in range(g * group, (g + 1) * group):
                qa = q_ref[0, pl.ds(q0, tq), a * d:(a + 1) * d]
                s = lax.dot_general(qa, kg, (((1,), (1,)), ((), ())), preferred_element_type=F32)
                s = jnp.where(mask, s, NEG_BIG)
                sk = sink_ref[:, a:a + 1]
                m = jnp.maximum(jnp.max(s, axis=-1, keepdims=True), sk)
                p = jnp.exp(s - m)
                den = jnp.sum(p, axis=-1, keepdims=True) + jnp.exp(sk - m)
                o = jnp.dot(p.astype(BF16), vg, preferred_element_type=F32) / den
                o_ref[0, pl.ds(q0, tq), a * d:(a + 1) * d] = o.astype(o_ref.dtype)
        return carry

    lax.fori_loop(0, length // tq, body, 0)


def _swa_attn(q, k, v, sink):
    b, length, nq = q.shape
    nk = k.shape[-1]
    tq, half = TQ_SWA, SWA_HALF_WINDOW
    whole = lambda bi: (bi, 0, 0)
    return pl.pallas_call(
        functools.partial(_swa_attn_kernel, length=length, tq=tq, win=tq + 2 * half, half=half,
                          q_heads=SWA_Q_HEADS, group=SWA_Q_HEADS // SWA_KV_HEADS, d=SWA_HEAD_DIM),
        out_shape=jax.ShapeDtypeStruct((b, length, nq), BF16),
        grid=(b,),
        in_specs=[pl.BlockSpec((1, length, nq), whole), pl.BlockSpec((1, length, nk), whole),
                  pl.BlockSpec((1, length, nk), whole), _resident(sink.shape)],
        out_specs=pl.BlockSpec((1, length, nq), whole),
        compiler_params=_params(1),
        name="swa_attn",
    )(q, k, v, sink)


def _dil_attn_kernel(*refs, seq, tq):
    qkv, o_ref = refs[:3 * DIL_GROUPS], refs[3 * DIL_GROUPS]
    acc_scr, lse_scr, v1_scr = refs[3 * DIL_GROUPS + 1:]
    d, heads = DIL_HEAD_DIM, DIL_HEADS_PER_GROUP
    slabs = DIL_GROUP_WIDTH // LANES
    nt = (((1,), (1,)), ((), ()))

    for gi, (window, dil) in enumerate(DIL_CONFIGS):
        q_ref, k_ref, v_ref = qkv[3 * gi:3 * gi + 3]
        length = seq // dil
        half = window // (2 * dil)
        t_q = min(tq, length)
        win = min(length, t_q + 2 * half)
        n_tiles = length // t_q

        def fill(r, carry, v_ref=v_ref, length=length):
            dst = pl.ds(pl.multiple_of(r * length, length), length)
            for hh in range(heads):
                v1_scr[dst, hh * LANES:hh * LANES + d] = v_ref[0, r, :, hh * d:(hh + 1) * d]
                v1_scr[dst, hh * LANES + d:(hh + 1) * LANES] = jnp.ones((length, d), BF16)
            return carry

        lax.fori_loop(0, dil, fill, 0)

        def tile(idx, carry, gi=gi, dil=dil, length=length, half=half, t_q=t_q, win=win,
                 n_tiles=n_tiles, q_ref=q_ref, k_ref=k_ref):
            r = idx // n_tiles
            q0 = pl.multiple_of((idx % n_tiles) * t_q, t_q)
            start = pl.multiple_of(jnp.clip(q0 - half, 0, length - win), half)
            dd = (lax.broadcasted_iota(jnp.int32, (t_q, win), 0)
                  - lax.broadcasted_iota(jnp.int32, (t_q, win), 1))
            bias = jnp.where(jnp.abs(dd + (q0 - start)) <= half, 0.0, NEG_BIG).astype(F32)
            rows = pl.ds(q0 * dil + r, t_q, stride=dil) if dil > 1 else pl.ds(q0, t_q)
            for c in range(slabs):
                o_pair, l_pair = [], []
                for hh in (2 * c, 2 * c + 1):
                    qa = q_ref[0, r, pl.ds(q0, t_q), hh * d:(hh + 1) * d]
                    kg = k_ref[0, r, pl.ds(start, win), hh * d:(hh + 1) * d]
                    v1 = v1_scr[pl.ds(pl.multiple_of(r * length + start, half), win),
                                hh * LANES:(hh + 1) * LANES]
                    s = lax.dot_general(qa, kg, nt, preferred_element_type=F32) + bias
                    m = jnp.max(s, axis=-1, keepdims=True)
                    p = jnp.exp2(s - m).astype(BF16)
                    o2 = jnp.dot(p, v1, preferred_element_type=F32)
                    den = pltpu.roll(o2, d, 1)
                    o_pair.append((o2 / den)[:, :d])
                    l_pair.append((m + jnp.log2(den))[:, :d])
                o_new = jnp.concatenate(o_pair, axis=-1)
                l_new = jnp.concatenate(l_pair, axis=-1)
                if gi == 0:
                    acc_scr[c, rows, :] = o_new
                    lse_scr[c, rows, :] = l_new
                else:
                    l_old = lse_scr[c, rows, :]
                    mx = jnp.maximum(l_old, l_new)
                    wa = jnp.exp2(l_old - mx)
                    wb = jnp.exp2(l_new - mx)
                    acc_scr[c, rows, :] = (wa * acc_scr[c, rows, :] + wb * o_new) / (wa + wb)
                    if gi + 1 < DIL_GROUPS:
                        lse_scr[c, rows, :] = mx + jnp.log2(wa + wb)
            return carry

        lax.fori_loop(0, dil * n_tiles, tile, 0)

    for c in range(slabs):
        o_ref[0, :, c * LANES:(c + 1) * LANES] = acc_scr[c].astype(o_ref.dtype)


def _dil_attn(qkv9, seq):
    b = qkv9[0].shape[0]
    in_specs = [pl.BlockSpec((1,) + t.shape[1:], lambda bi: (bi, 0, 0, 0), pipeline_mode=pl.Buffered(1))
                for t in qkv9]
    slabs = DIL_GROUP_WIDTH // LANES
    return pl.pallas_call(
        functools.partial(_dil_attn_kernel, seq=seq, tq=TQ_DIL),
        out_shape=jax.ShapeDtypeStruct((b, seq, DIL_GROUP_WIDTH), BF16),
        grid=(b,),
        in_specs=in_specs,
        out_specs=pl.BlockSpec((1, seq, DIL_GROUP_WIDTH), lambda bi: (bi, 0, 0)),
        scratch_shapes=[pltpu.VMEM((slabs, seq, LANES), F32), pltpu.VMEM((slabs, seq, LANES), F32),
                        pltpu.VMEM((seq, DIL_HEADS_PER_GROUP * LANES), BF16)],
        compiler_params=_params(1),
        name="dil_attn",
    )(*qkv9)


def _out_ffn_kernel(x_ref, o_ref, wo_ref, fn_ref, wg_ref, wu_ref, wd_ref, y_ref):
    x1 = x_ref[...] + jnp.dot(o_ref[...], wo_ref[...], preferred_element_type=F32)
    h = _rms(x1, fn_ref[...]).astype(BF16)
    gate = jnp.dot(h, wg_ref[...], preferred_element_type=F32)
    up = jnp.dot(h, wu_ref[...], preferred_element_type=F32)
    act = (gate * jax.nn.sigmoid(gate) * up).astype(BF16)
    y_ref[...] = x1 + jnp.dot(act, wd_ref[...], preferred_element_type=F32)


def _out_ffn(x, o, wo, fn, wg, wu, wd):
    m = x.shape[0]
    tm = TM_FFN
    row = lambda i: (i, 0)
    in_specs = [pl.BlockSpec((tm, D_MODEL), row), pl.BlockSpec((tm, o.shape[1]), row),
                _resident(wo.shape), _resident(fn.shape), _resident(wg.shape), _resident(wu.shape),
                _resident(wd.shape)]
    return pl.pallas_call(
        _out_ffn_kernel,
        out_shape=jax.ShapeDtypeStruct((m, D_MODEL), F32),
        grid=(m // tm,),
        in_specs=in_specs,
        out_specs=pl.BlockSpec((tm, D_MODEL), row),
        compiler_params=_params(1),
        name="out_ffn",
    )(x, o, wo, fn, wg, wu, wd)


def _rope_tables(seq, dim, lanes):
    inv = 1.0 / (ROPE_THETA ** (jnp.arange(0, dim, 2, dtype=F32) / dim))
    ang = jnp.arange(seq, dtype=F32)[:, None] * inv[None, :]
    cos, sin = jnp.cos(ang), jnp.sin(ang)
    zero = jnp.zeros_like(sin)
    tabs = (jnp.concatenate([cos, cos], -1), jnp.concatenate([-sin, zero], -1),
            jnp.concatenate([zero, sin], -1))
    return tuple(jnp.tile(t, (1, lanes // dim)) for t in tabs)


def _row(v):
    return v.reshape(1, -1).astype(F32)


def _head_gain(g, reps, scale=1.0):
    return jnp.tile(g.astype(F32) * scale, reps).reshape(1, -1)


def _trunk(x, attn_norm, ffn_norm, w_gate, w_up, w_down,
           mla_wq_a, mla_q_a_norm, mla_wq_b, mla_wkv_a, mla_kv_a_norm, mla_wkv_b,
           mla_q_norm, mla_k_norm, mla_wo,
           swa_wqkv, swa_q_norm, swa_k_norm, swa_sink, swa_wo,
           dil_wqkv, dil_q_norm, dil_k_norm, dil_wo):
    b, seq, _ = x.shape
    m = b * seq
    xf = x.reshape(m, D_MODEL)
    tabs64 = _rope_tables(seq, HEAD_DIM, LANES)
    cos, msin, psin = _rope_tables(seq, MLA_ROPE, MLA_ROPE)
    pad = ((0, 0), (0, LANES - MLA_ROPE))
    tabs_mla = (jnp.pad(cos, pad), jnp.pad(msin, pad), jnp.pad(psin, pad))
    blk = jnp.arange(NORM_CHUNK) // HEAD_DIM
    gmat = ((blk[:, None] == blk[None, :]).astype(F32) / HEAD_DIM).astype(BF16)
    reps = NORM_CHUNK // HEAD_DIM

    for i in range(DEPTH):
        kind, j = i % N_MIXERS, i // N_MIXERS
        an = _row(attn_norm[i])
        if kind == 0:
            wqb = mla_wq_b[j].reshape(MLA_Q_LORA, MLA_HEADS, MLA_QK)
            wqb = jnp.pad(wqb, ((0, 0), (0, 0), (0, MLA_QK_PAD - MLA_QK))).reshape(MLA_Q_LORA, -1)
            wkva = jnp.pad(mla_wkv_a[j], ((0, 0), (0, LANES - MLA_ROPE)))
            gpad = (0, MLA_QK_PAD - MLA_QK)
            q, k, v = _mla_proj(
                xf, seq, an, mla_wq_a[j].astype(BF16), _row(mla_q_a_norm[j]), wqb.astype(BF16),
                wkva.astype(BF16), _row(mla_kv_a_norm[j]), mla_wkv_b[j].astype(BF16),
                _row(jnp.pad(mla_q_norm[j], gpad)), _row(jnp.pad(mla_k_norm[j], gpad)), tabs_mla)
            o = _mla_attn(q.reshape(b, seq, -1), k.reshape(b, seq, -1), v.reshape(b, seq, -1))
            wo = mla_wo[j]
        elif kind == 1:
            nq, nk = SWA_Q_HEADS * SWA_HEAD_DIM, SWA_KV_HEADS * SWA_HEAD_DIM
            q, k, v = _swa_proj(xf, seq, an, swa_wqkv[j].astype(BF16),
                                _head_gain(swa_q_norm[j], reps, SWA_HEAD_DIM ** -0.5),
                                _head_gain(swa_k_norm[j], reps), gmat, tabs64, nq, nk)
            o = _swa_attn(q.reshape(b, seq, nq), k.reshape(b, seq, nk), v.reshape(b, seq, nk),
                          _row(swa_sink[j]))
            wo = swa_wo[j]
        else:
            qkv9 = _dil_proj(xf, b, seq, an, dil_wqkv[j].astype(BF16),
                             _head_gain(dil_q_norm[j], reps, DIL_HEAD_DIM ** -0.5 * LOG2E),
                             _head_gain(dil_k_norm[j], reps), gmat, tabs64)
            o = _dil_attn(qkv9, seq)
            wo = dil_wo[j]
        xf = _out_ffn(xf, o.reshape(m, -1), wo.astype(BF16), _row(ffn_norm[i]), w_gate[i].astype(BF16),
                      w_up[i].astype(BF16), w_down[i].astype(BF16))
    return xf.reshape(b, seq, D_MODEL)


def kernel(x_prompt, x_sample, attn_norm, ffn_norm, w_gate, w_up, w_down, mla_wq_a, mla_q_a_norm, mla_wq_b, mla_wkv_a, mla_kv_a_norm, mla_wkv_b, mla_q_norm, mla_k_norm, mla_wo, swa_wqkv, swa_q_norm, swa_k_norm, swa_sink, swa_wo, dil_wqkv, dil_q_norm, dil_k_norm, dil_wo):
    weights = (attn_norm, ffn_norm, w_gate, w_up, w_down,
               mla_wq_a, mla_q_a_norm, mla_wq_b, mla_wkv_a, mla_kv_a_norm, mla_wkv_b,
               mla_q_norm, mla_k_norm, mla_wo,
               swa_wqkv, swa_q_norm, swa_k_norm, swa_sink, swa_wo,
               dil_wqkv, dil_q_norm, dil_k_norm, dil_wo)
    return (_trunk(x_prompt, *weights), _trunk(x_sample, *weights))
```

```python
import functools
import math

import jax
import jax.numpy as jnp
from jax import lax
from jax.experimental import pallas as pl
from jax.experimental.pallas import tpu as pltpu

F32 = jnp.float32
BF16 = jnp.bfloat16

D_MODEL = 1024
DEPTH = 4
N_MIXERS = 3
ROPE_THETA = 10000.0
NORM_EPS = 1e-6
NEG_BIG = -1e30
LOG2E = math.log2(math.e)

MLA_HEADS = 8
MLA_Q_LORA = 384
MLA_KV_LORA = 256
MLA_NOPE = 128
MLA_ROPE = 64
MLA_V = 128
MLA_QK = MLA_NOPE + MLA_ROPE
MLA_QK_PAD = 256

SWA_Q_HEADS = 16
SWA_KV_HEADS = 4
SWA_HEAD_DIM = 64
SWA_HALF_WINDOW = 128

DIL_CONFIGS = ((128, 1), (512, 4), (2048, 16))
DIL_GROUPS = len(DIL_CONFIGS)
DIL_HEADS_PER_GROUP = 8
DIL_HEAD_DIM = 64
DIL_GROUP_WIDTH = DIL_HEADS_PER_GROUP * DIL_HEAD_DIM

D_FF = 2816

LANES = 128
HEAD_DIM = 64
NORM_CHUNK = 256
VMEM_LIMIT = 56 * 1024 * 1024

TM_PROJ = 512
SUB_PROJ = 256
TM_FFN = 512
SUB_FFN = 256
TQ_MLA = 256
TQ_BAND = 256
PIPE_DEPTH = 3
VT_ROWS = 80
NT_DIMS = (((1,), (1,)), ((), ()))


def _rms(x, g):
    return x * lax.rsqrt(jnp.mean(x * x, axis=-1, keepdims=True) + NORM_EPS) * g


def _rope(v, c, s1, s2):
    return v * c + pltpu.roll(v, LANES - 32, 1) * s1 + pltpu.roll(v, 32, 1) * s2


def _resident(shape):
    return pl.BlockSpec(shape, lambda *_: (0,) * len(shape), pipeline_mode=pl.Buffered(1))


def _params(n_axes):
    return pltpu.CompilerParams(dimension_semantics=("parallel",) * n_axes,
                                vmem_limit_bytes=VMEM_LIMIT)


def _mla_proj_kernel(x_ref, an_ref, wqa_ref, qan_ref, wqb_ref, wkva_ref, kvan_ref, wkvb_ref,
                     gq_ref, gk_ref, c_ref, s1_ref, s2_ref, q_out, k_out, v_out):
    gq, gk = gq_ref[...], gk_ref[...]
    scale = MLA_QK ** -0.5 * LOG2E

    def project(rows):
        h = _rms(x_ref[rows, :], an_ref[...]).astype(BF16)
        cq = jnp.dot(h, wqa_ref[...], preferred_element_type=F32)
        cq = _rms(cq, qan_ref[...]).astype(BF16)
        qf = jnp.dot(cq, wqb_ref[...], preferred_element_type=F32)
        kva = jnp.dot(h, wkva_ref[...], preferred_element_type=F32)
        ckv = _rms(kva[:, :MLA_KV_LORA], kvan_ref[...]).astype(BF16)
        kv = jnp.dot(ckv, wkvb_ref[...], preferred_element_type=F32)
        return qf, kva, kv

    for r0 in range(0, TM_PROJ, SUB_PROJ):
        rows = slice(r0, r0 + SUB_PROJ)
        qf, kva, kv = project(rows)
        c, s1, s2 = c_ref[rows, :], s1_ref[rows, :], s2_ref[rows, :]
        kr = kva[:, MLA_KV_LORA:]
        kr_ss = jnp.sum(kr * kr, axis=-1, keepdims=True)
        kr_roped = _rope(kr * gk[:, LANES:], c, s1, s2)
        for hh in range(MLA_HEADS):
            lo = hh * MLA_QK_PAD
            qh = qf[:, lo:lo + MLA_QK_PAD]
            rq = lax.rsqrt(jnp.sum(qh * qh, axis=-1, keepdims=True) * (1.0 / MLA_QK) + NORM_EPS) * scale
            qn = qh * rq * gq
            q_out[rows, lo:lo + LANES] = qn[:, :LANES].astype(BF16)
            q_out[rows, lo + LANES:lo + MLA_QK_PAD] = _rope(qn[:, LANES:], c, s1, s2).astype(BF16)
            kn = kv[:, lo:lo + LANES]
            rk = lax.rsqrt((jnp.sum(kn * kn, axis=-1, keepdims=True) + kr_ss) * (1.0 / MLA_QK) + NORM_EPS)
            k_out[rows, lo:lo + LANES] = (kn * rk * gk[:, :LANES]).astype(BF16)
            k_out[rows, lo + LANES:lo + MLA_QK_PAD] = (kr_roped * rk).astype(BF16)
            v_out[rows, hh * MLA_V:(hh + 1) * MLA_V] = kv[:, lo + LANES:lo + MLA_QK_PAD].astype(BF16)


def _mla_proj(x, seq, an, wqa, qan, wqb, wkva, kvan, wkvb, gq, gk, tabs):
    m = x.shape[0]
    tm = TM_PROJ
    nblk = seq // tm
    row = lambda i: (i, 0)
    pos = lambda i: (i % nblk, 0)
    hq = MLA_HEADS * MLA_QK_PAD
    in_specs = [pl.BlockSpec((tm, D_MODEL), row), _resident(an.shape), _resident(wqa.shape),
                _resident(qan.shape), _resident(wqb.shape), _resident(wkva.shape),
                _resident(kvan.shape), _resident(wkvb.shape), _resident(gq.shape), _resident(gk.shape)]
    in_specs += [pl.BlockSpec((tm, LANES), pos)] * 3
    return pl.pallas_call(
        _mla_proj_kernel,
        out_shape=(jax.ShapeDtypeStruct((m, hq), BF16), jax.ShapeDtypeStruct((m, hq), BF16),
                   jax.ShapeDtypeStruct((m, MLA_HEADS * MLA_V), BF16)),
        grid=(m // tm,),
        in_specs=in_specs,
        out_specs=(pl.BlockSpec((tm, hq), row), pl.BlockSpec((tm, hq), row),
                   pl.BlockSpec((tm, MLA_HEADS * MLA_V), row)),
        compiler_params=_params(1),
        name="mla_proj",
    )(x, an, wqa, qan, wqb, wkva, kvan, wkvb, gq, gk, *tabs)


def _mla_attn_kernel(q_ref, k_ref, v_ref, o_ref, v1_scr, *, seq, tq):
    v1_scr[:, :MLA_V] = v_ref[0]
    v1_scr[:, MLA_V:] = jnp.ones((seq, MLA_V), BF16)

    def scores(t):
        return lax.dot_general(q_ref[0, t * tq:(t + 1) * tq, :], k_ref[0], NT_DIMS,
                               preferred_element_type=F32)

    def finish(s, t):
        m = jnp.max(s, axis=-1, keepdims=True)
        p = jnp.exp2(s - m).astype(BF16)
        o = jnp.dot(p, v1_scr[...], preferred_element_type=F32)
        o_ref[0, t * tq:(t + 1) * tq, :] = (o[:, :MLA_V] / o[:, MLA_V:]).astype(o_ref.dtype)

    n_tiles = seq // tq
    queue = [scores(t) for t in range(PIPE_DEPTH)]
    for t in range(n_tiles):
        s = queue.pop(0)
        if t + PIPE_DEPTH < n_tiles:
            queue.append(scores(t + PIPE_DEPTH))
        finish(s, t)


def _mla_attn(q, k, v):
    b, seq, _ = q.shape
    return pl.pallas_call(
        functools.partial(_mla_attn_kernel, seq=seq, tq=TQ_MLA),
        out_shape=jax.ShapeDtypeStruct((b, seq, MLA_HEADS * MLA_V), BF16),
        grid=(b, MLA_HEADS),
        in_specs=[pl.BlockSpec((1, seq, MLA_QK_PAD), lambda bi, hi: (bi, 0, hi)),
                  pl.BlockSpec((1, seq, MLA_QK_PAD), lambda bi, hi: (bi, 0, hi)),
                  pl.BlockSpec((1, seq, MLA_V), lambda bi, hi: (bi, 0, hi))],
        out_specs=pl.BlockSpec((1, seq, MLA_V), lambda bi, hi: (bi, 0, hi)),
        scratch_shapes=[pltpu.VMEM((seq, 2 * MLA_V), BF16)],
        compiler_params=_params(2),
        name="mla_attn",
    )(q, k, v)


def _normed_roped_chunks(qkv, col0, width, g, gmat, c, s1, s2):
    for j in range(width // NORM_CHUNK):
        t = qkv[:, col0 + j * NORM_CHUNK:col0 + (j + 1) * NORM_CHUNK]
        ms = jnp.dot((t * t).astype(BF16), gmat, preferred_element_type=F32)
        tn = t * lax.rsqrt(ms + NORM_EPS) * g
        for half in range(NORM_CHUNK // LANES):
            yield (j * NORM_CHUNK + half * LANES,
                   _rope(tn[:, half * LANES:(half + 1) * LANES], c, s1, s2))


def _projected_subtiles(x_ref, an_ref, w_ref):
    for r0 in range(0, TM_PROJ, SUB_PROJ):
        rows = slice(r0, r0 + SUB_PROJ)
        h = _rms(x_ref[rows, :], an_ref[...]).astype(BF16)
        yield rows, jnp.dot(h, w_ref[...], preferred_element_type=F32)


def _swa_proj_kernel(x_ref, an_ref, w_ref, gq_ref, gk_ref, gmat_ref, c_ref, s1_ref, s2_ref,
                     q_out, k_out, v_out, *, nq, nk):
    gmat = gmat_ref[...]
    for rows, qkv in _projected_subtiles(x_ref, an_ref, w_ref):
        tabs = (c_ref[rows, :], s1_ref[rows, :], s2_ref[rows, :])
        for lo, val in _normed_roped_chunks(qkv, 0, nq, gq_ref[...], gmat, *tabs):
            q_out[rows, lo:lo + LANES] = val.astype(BF16)
        for lo, val in _normed_roped_chunks(qkv, nq, nk, gk_ref[...], gmat, *tabs):
            k_out[rows, lo:lo + LANES] = val.astype(BF16)
        v_out[rows, :] = qkv[:, nq + nk:].astype(BF16)


def _swa_proj(x, seq, an, w, gq, gk, gmat, tabs, nq, nk):
    m = x.shape[0]
    tm = TM_PROJ
    nblk = seq // tm
    nv = w.shape[1] - nq - nk
    row = lambda i: (i, 0)
    pos = lambda i: (i % nblk, 0)
    in_specs = [pl.BlockSpec((tm, D_MODEL), row), _resident(an.shape), _resident(w.shape),
                _resident(gq.shape), _resident(gk.shape), _resident(gmat.shape)]
    in_specs += [pl.BlockSpec((tm, LANES), pos)] * 3
    return pl.pallas_call(
        functools.partial(_swa_proj_kernel, nq=nq, nk=nk),
        out_shape=(jax.ShapeDtypeStruct((m, nq), BF16), jax.ShapeDtypeStruct((m, nk), BF16),
                   jax.ShapeDtypeStruct((m, nv), BF16)),
        grid=(m // tm,),
        in_specs=in_specs,
        out_specs=(pl.BlockSpec((tm, nq), row), pl.BlockSpec((tm, nk), row), pl.BlockSpec((tm, nv), row)),
        compiler_params=_params(1),
        name="swa_proj",
    )(x, an, w, gq, gk, gmat, *tabs)


def _dil_proj_kernel(x_ref, an_ref, w_ref, gq_ref, gk_ref, gmat_ref, c_ref, s1_ref, s2_ref, *refs):
    outs, stage = refs[:3 * DIL_GROUPS], refs[3 * DIL_GROUPS]
    gmat = gmat_ref[...]
    nq = DIL_GROUPS * DIL_GROUP_WIDTH
    slabs_per_group = DIL_GROUP_WIDTH // LANES
    for rows, qkv in _projected_subtiles(x_ref, an_ref, w_ref):
        r0 = rows.start
        sub = r0 // SUB_PROJ
        tabs = (c_ref[rows, :], s1_ref[rows, :], s2_ref[rows, :])

        def emit(tensor, lo, val, sub=sub, r0=r0):
            gi, c = lo // DIL_GROUP_WIDTH, (lo % DIL_GROUP_WIDTH) // LANES
            dil = DIL_CONFIGS[gi][1]
            out = outs[3 * gi + tensor]
            cols = slice(c * LANES, (c + 1) * LANES)
            if dil == 1:
                out[0, 0, r0:r0 + SUB_PROJ, cols] = val.astype(BF16)
                return
            slot = ((sub * 3 + tensor) * (DIL_GROUPS - 1) + gi - 1) * slabs_per_group + c
            stage[slot] = val
            n = SUB_PROJ // dil
            for r in range(dil):
                out[0, r, r0 // dil:r0 // dil + n, cols] = stage[slot, pl.ds(r, n, stride=dil), :].astype(BF16)

        for lo, val in _normed_roped_chunks(qkv, 0, nq, gq_ref[...], gmat, *tabs):
            emit(0, lo, val)
        for lo, val in _normed_roped_chunks(qkv, nq, nq, gk_ref[...], gmat, *tabs):
            emit(1, lo, val)
        for lo in range(0, nq, LANES):
            emit(2, lo, qkv[:, 2 * nq + lo:2 * nq + lo + LANES])


def _dil_proj(x, b, seq, an, w, gq, gk, gmat, tabs):
    m = x.shape[0]
    tm = TM_PROJ
    nblk = seq // tm
    row = lambda i: (i, 0)
    pos = lambda i: (i % nblk, 0)
    in_specs = [pl.BlockSpec((tm, D_MODEL), row), _resident(an.shape), _resident(w.shape),
                _resident(gq.shape), _resident(gk.shape), _resident(gmat.shape)]
    in_specs += [pl.BlockSpec((tm, LANES), pos)] * 3
    out_shape, out_specs = [], []
    for _, dil in DIL_CONFIGS:
        for _ in range(3):
            out_shape.append(jax.ShapeDtypeStruct((b, dil, seq // dil, DIL_GROUP_WIDTH), BF16))
            out_specs.append(pl.BlockSpec((1, dil, tm // dil, DIL_GROUP_WIDTH),
                                          lambda i: (i // nblk, 0, i % nblk, 0)))
    n_slots = (TM_PROJ // SUB_PROJ) * 3 * (DIL_GROUPS - 1) * (DIL_GROUP_WIDTH // LANES)
    return pl.pallas_call(
        _dil_proj_kernel,
        out_shape=tuple(out_shape),
        grid=(m // tm,),
        in_specs=in_specs,
        out_specs=tuple(out_specs),
        scratch_shapes=[pltpu.VMEM((n_slots, SUB_PROJ, LANES), F32)],
        compiler_params=_params(1),
        name="dil_proj",
    )(x, an, w, gq, gk, gmat, *tabs)


def _band_bias_t(win, tq, off, half):
    dd = (lax.broadcasted_iota(jnp.int32, (win, tq), 0)
          - lax.broadcasted_iota(jnp.int32, (win, tq), 1))
    return jnp.where(jnp.abs(dd + off) <= half, 0.0, NEG_BIG).astype(F32)


def _fill_vt(vt_scr, v_block, head0, cols):
    d = HEAD_DIM
    vt = v_block.astype(F32).T.astype(BF16)
    for e in range(LANES // d):
        lo = (head0 + e) * VT_ROWS
        vt_scr[lo:lo + d, cols] = vt[e * d:(e + 1) * d, :]
        vt_scr[lo + d:lo + VT_ROWS, cols] = jnp.ones((VT_ROWS - d, vt.shape[1]), BF16)


def _probs_t(st, sk2):
    m = jnp.max(st, axis=0, keepdims=True)
    if sk2 is not None:
        m = jnp.maximum(m, sk2)
    return jnp.exp2(st - m).astype(BF16), m


def _swa_attn_kernel(q_ref, k_ref, v_ref, sink_ref, o_ref, vt_scr, *, length, tq, win, half, q_heads, group):
    d = HEAD_DIM
    chunk = 4 * LANES
    for j in range(q_heads // group * d // LANES):
        for c0 in range(0, length, chunk):
            _fill_vt(vt_scr, v_ref[0, c0:c0 + chunk, j * LANES:(j + 1) * LANES], 2 * j, slice(c0, c0 + chunk))

    def body(i, carry):
        q0 = pl.multiple_of(i * tq, tq)
        start = pl.multiple_of(jnp.clip(q0 - half, 0, length - win), LANES)
        bias_t = _band_bias_t(win, tq, start - q0, half)

        def scores(a):
            kg = k_ref[0, pl.ds(start, win), (a // group) * d:(a // group + 1) * d]
            qa = q_ref[0, pl.ds(q0, tq), a * d:(a + 1) * d]
            return lax.dot_general(kg, qa, NT_DIMS, preferred_element_type=F32) + bias_t

        queue = [scores(a) for a in range(PIPE_DEPTH)]
        pair = []
        for a in range(q_heads):
            st = queue.pop(0)
            if a + PIPE_DEPTH < q_heads:
                queue.append(scores(a + PIPE_DEPTH))
            sk2 = sink_ref[:, a:a + 1]
            pt, m = _probs_t(st, sk2)
            g = a // group
            o2t = jnp.dot(vt_scr[g * VT_ROWS:(g + 1) * VT_ROWS, pl.ds(start, win)], pt,
                          preferred_element_type=F32)
            pair.append(o2t[:d, :] / (o2t[d:d + 1, :] + jnp.exp2(sk2 - m)))
            if len(pair) == LANES // d:
                o_ref[0, pl.ds(q0, tq), (a - 1) * d:(a + 1) * d] = (
                    jnp.concatenate(pair, axis=0).T.astype(o_ref.dtype))
                pair = []
        return carry

    lax.fori_loop(0, length // tq, body, 0)


def _swa_attn(q, k, v, sink):
    b, length, nq = q.shape
    nk = k.shape[-1]
    tq, half = TQ_BAND, SWA_HALF_WINDOW
    whole = lambda bi: (bi, 0, 0)
    return pl.pallas_call(
        functools.partial(_swa_attn_kernel, length=length, tq=tq, win=tq + 2 * half, half=half,
                          q_heads=SWA_Q_HEADS, group=SWA_Q_HEADS // SWA_KV_HEADS),
        out_shape=jax.ShapeDtypeStruct((b, length, nq), BF16),
        grid=(b,),
        in_specs=[pl.BlockSpec((1, length, nq), whole), pl.BlockSpec((1, length, nk), whole),
                  pl.BlockSpec((1, length, nk), whole), _resident(sink.shape)],
        out_specs=pl.BlockSpec((1, length, nq), whole),
        scratch_shapes=[pltpu.VMEM((SWA_KV_HEADS * VT_ROWS, length), BF16)],
        compiler_params=_params(1),
        name="swa_attn",
    )(q, k, v, sink)


def _dil_attn_kernel(*refs, seq, tq):
    qkv, o_ref = refs[:3 * DIL_GROUPS], refs[3 * DIL_GROUPS]
    acc_scr, lse_scr, vt_scr = refs[3 * DIL_GROUPS + 1:]
    d, heads = DIL_HEAD_DIM, DIL_HEADS_PER_GROUP
    slabs = DIL_GROUP_WIDTH // LANES

    for gi, (window, dil) in enumerate(DIL_CONFIGS):
        q_ref, k_ref, v_ref = qkv[3 * gi:3 * gi + 3]
        length = seq // dil
        half = window // (2 * dil)
        t_q = min(tq, length)
        win = min(length, t_q + 2 * LANES)
        n_tiles = length // t_q

        def fill(r, carry, v_ref=v_ref, length=length):
            cols = pl.ds(pl.multiple_of(r * length, LANES), length)
            for j in range(slabs):
                _fill_vt(vt_scr, v_ref[0, r, :, j * LANES:(j + 1) * LANES], 2 * j, cols)
            return carry

        lax.fori_loop(0, dil, fill, 0)

        def tile(idx, carry, gi=gi, dil=dil, length=length, half=half, t_q=t_q, win=win,
                 n_tiles=n_tiles, q_ref=q_ref, k_ref=k_ref):
            r = idx // n_tiles
            q0 = pl.multiple_of((idx % n_tiles) * t_q, t_q)
            start = pl.multiple_of(jnp.clip(q0 - LANES, 0, length - win), LANES)
            bias_t = _band_bias_t(win, t_q, start - q0, half)
            rows = pl.ds(q0 * dil + r, t_q, stride=dil) if dil > 1 else pl.ds(q0, t_q)
            vcols = pl.ds(pl.multiple_of(r * length + start, LANES), win)

            def scores(hh):
                qa = q_ref[0, r, pl.ds(q0, t_q), hh * d:(hh + 1) * d]
                kg = k_ref[0, r, pl.ds(start, win), hh * d:(hh + 1) * d]
                return lax.dot_general(kg, qa, NT_DIMS, preferred_element_type=F32) + bias_t

            def merge(c, o_new, l_new):
                if gi == 0:
                    acc_scr[c, rows, :] = o_new
                    lse_scr[c, rows, :] = l_new
                    return
                l_old = lse_scr[c, rows, :]
                mx = jnp.maximum(l_old, l_new)
                wa = jnp.exp2(l_old - mx)
                wb = jnp.exp2(l_new - mx)
                acc_scr[c, rows, :] = (wa * acc_scr[c, rows, :] + wb * o_new) / (wa + wb)
                if gi + 1 < DIL_GROUPS:
                    lse_scr[c, rows, :] = mx + jnp.log2(wa + wb)

            queue = [scores(hh) for hh in range(PIPE_DEPTH)]
            o_pair, l_pair = [], []
            for hh in range(heads):
                st = queue.pop(0)
                if hh + PIPE_DEPTH < heads:
                    queue.append(scores(hh + PIPE_DEPTH))
                pt, m = _probs_t(st, None)
                o2t = jnp.dot(vt_scr[hh * VT_ROWS:(hh + 1) * VT_ROWS, vcols], pt,
                              preferred_element_type=F32)
                den = o2t[d:d + 1, :]
                o_pair.append(o2t[:d, :] / den)
                l_pair.append(jnp.broadcast_to(m + jnp.log2(den), (d, t_q)))
                if len(o_pair) == LANES // d:
                    merge(hh // 2, jnp.concatenate(o_pair, axis=0).T, jnp.concatenate(l_pair, axis=0).T)
                    o_pair, l_pair = [], []
            return carry

        lax.fori_loop(0, dil * n_tiles, tile, 0)

    for c in range(slabs):
        o_ref[0, :, c * LANES:(c + 1) * LANES] = acc_scr[c].astype(o_ref.dtype)


def _dil_attn(qkv9, seq):
    b = qkv9[0].shape[0]
    in_specs = [pl.BlockSpec((1,) + t.shape[1:], lambda bi: (bi, 0, 0, 0), pipeline_mode=pl.Buffered(1))
                for t in qkv9]
    slabs = DIL_GROUP_WIDTH // LANES
    return pl.pallas_call(
        functools.partial(_dil_attn_kernel, seq=seq, tq=TQ_BAND),
        out_shape=jax.ShapeDtypeStruct((b, seq, DIL_GROUP_WIDTH), BF16),
        grid=(b,),
        in_specs=in_specs,
        out_specs=pl.BlockSpec((1, seq, DIL_GROUP_WIDTH), lambda bi: (bi, 0, 0)),
        scratch_shapes=[pltpu.VMEM((slabs, seq, LANES), F32), pltpu.VMEM((slabs, seq, LANES), F32),
                        pltpu.VMEM((DIL_HEADS_PER_GROUP * VT_ROWS, seq), BF16)],
        compiler_params=_params(1),
        name="dil_attn",
    )(*qkv9)


def _out_ffn_kernel(x_ref, o_ref, wo_ref, fn_ref, wg_ref, wu_ref, wd_ref, y_ref):
    subs = [slice(r0, r0 + SUB_FFN) for r0 in range(0, TM_FFN, SUB_FFN)]
    x1s = [x_ref[rows, :] + jnp.dot(o_ref[rows, :], wo_ref[...], preferred_element_type=F32) for rows in subs]
    hs = [_rms(x1, fn_ref[...]).astype(BF16) for x1 in x1s]
    gate_up = [(jnp.dot(h, wg_ref[...], preferred_element_type=F32),
                jnp.dot(h, wu_ref[...], preferred_element_type=F32)) for h in hs]
    acts = [(gate * jax.nn.sigmoid(gate) * up).astype(BF16) for gate, up in gate_up]
    for rows, x1, act in zip(subs, x1s, acts):
        y_ref[rows, :] = x1 + jnp.dot(act, wd_ref[...], preferred_element_type=F32)


def _out_ffn(x, o, wo, fn, wg, wu, wd):
    m = x.shape[0]
    tm = TM_FFN
    row = lambda i: (i, 0)
    in_specs = [pl.BlockSpec((tm, D_MODEL), row), pl.BlockSpec((tm, o.shape[1]), row),
                _resident(wo.shape), _resident(fn.shape), _resident(wg.shape), _resident(wu.shape),
                _resident(wd.shape)]
    return pl.pallas_call(
        _out_ffn_kernel,
        out_shape=jax.ShapeDtypeStruct((m, D_MODEL), F32),
        grid=(m // tm,),
        in_specs=in_specs,
        out_specs=pl.BlockSpec((tm, D_MODEL), row),
        compiler_params=_params(1),
        name="out_ffn",
    )(x, o, wo, fn, wg, wu, wd)


def _rope_tables(seq, dim, lanes):
    inv = 1.0 / (ROPE_THETA ** (jnp.arange(0, dim, 2, dtype=F32) / dim))
    ang = jnp.arange(seq, dtype=F32)[:, None] * inv[None, :]
    cos, sin = jnp.cos(ang), jnp.sin(ang)
    zero = jnp.zeros_like(sin)
    tabs = (jnp.concatenate([cos, cos], -1), jnp.concatenate([-sin, zero], -1),
            jnp.concatenate([zero, sin], -1))
    return tuple(jnp.tile(t, (1, lanes // dim)) for t in tabs)


def _row(v):
    return v.reshape(1, -1).astype(F32)


def _head_gain(g, reps, scale=1.0):
    return jnp.tile(g.astype(F32) * scale, reps).reshape(1, -1)


def _trunk(x, attn_norm, ffn_norm, w_gate, w_up, w_down,
           mla_wq_a, mla_q_a_norm, mla_wq_b, mla_wkv_a, mla_kv_a_norm, mla_wkv_b,
           mla_q_norm, mla_k_norm, mla_wo,
           swa_wqkv, swa_q_norm, swa_k_norm, swa_sink, swa_wo,
           dil_wqkv, dil_q_norm, dil_k_norm, dil_wo):
    b, seq, _ = x.shape
    m = b * seq
    xf = x.reshape(m, D_MODEL)
    tabs64 = _rope_tables(seq, HEAD_DIM, LANES)
    cos, msin, psin = _rope_tables(seq, MLA_ROPE, MLA_ROPE)
    pad = ((0, 0), (0, LANES - MLA_ROPE))
    tabs_mla = (jnp.pad(cos, pad), jnp.pad(msin, pad), jnp.pad(psin, pad))
    blk = jnp.arange(NORM_CHUNK) // HEAD_DIM
    gmat = ((blk[:, None] == blk[None, :]).astype(F32) / HEAD_DIM).astype(BF16)
    reps = NORM_CHUNK // HEAD_DIM

    for i in range(DEPTH):
        kind, j = i % N_MIXERS, i // N_MIXERS
        an = _row(attn_norm[i])
        if kind == 0:
            wqb = mla_wq_b[j].reshape(MLA_Q_LORA, MLA_HEADS, MLA_QK)
            wqb = jnp.pad(wqb, ((0, 0), (0, 0), (0, MLA_QK_PAD - MLA_QK))).reshape(MLA_Q_LORA, -1)
            wkva = jnp.pad(mla_wkv_a[j], ((0, 0), (0, LANES - MLA_ROPE)))
            gpad = (0, MLA_QK_PAD - MLA_QK)
            q, k, v = _mla_proj(
                xf, seq, an, mla_wq_a[j].astype(BF16), _row(mla_q_a_norm[j]), wqb.astype(BF16),
                wkva.astype(BF16), _row(mla_kv_a_norm[j]), mla_wkv_b[j].astype(BF16),
                _row(jnp.pad(mla_q_norm[j], gpad)), _row(jnp.pad(mla_k_norm[j], gpad)), tabs_mla)
            o = _mla_attn(q.reshape(b, seq, -1), k.reshape(b, seq, -1), v.reshape(b, seq, -1))
            wo = mla_wo[j]
        elif kind == 1:
            nq, nk = SWA_Q_HEADS * SWA_HEAD_DIM, SWA_KV_HEADS * SWA_HEAD_DIM
            q, k, v = _swa_proj(xf, seq, an, swa_wqkv[j].astype(BF16),
                                _head_gain(swa_q_norm[j], reps, SWA_HEAD_DIM ** -0.5 * LOG2E),
                                _head_gain(swa_k_norm[j], reps), gmat, tabs64, nq, nk)
            o = _swa_attn(q.reshape(b, seq, nq), k.reshape(b, seq, nk), v.reshape(b, seq, nk),
                          _row(swa_sink[j]) * LOG2E)
            wo = swa_wo[j]
        else:
            qkv9 = _dil_proj(xf, b, seq, an, dil_wqkv[j].astype(BF16),
                             _head_gain(dil_q_norm[j], reps, DIL_HEAD_DIM ** -0.5 * LOG2E),
                             _head_gain(dil_k_norm[j], reps), gmat, tabs64)
            o = _dil_attn(qkv9, seq)
            wo = dil_wo[j]
        xf = _out_ffn(xf, o.reshape(m, -1), wo.astype(BF16), _row(ffn_norm[i]), w_gate[i].astype(BF16),
                      w_up[i].astype(BF16), w_down[i].astype(BF16))
    return xf.reshape(b, seq, D_MODEL)


def kernel(x_prompt, x_sample, attn_norm, ffn_norm, w_gate, w_up, w_down, mla_wq_a, mla_q_a_norm, mla_wq_b, mla_wkv_a, mla_kv_a_norm, mla_wkv_b, mla_q_norm, mla_k_norm, mla_wo, swa_wqkv, swa_q_norm, swa_k_norm, swa_sink, swa_wo, dil_wqkv, dil_q_norm, dil_k_norm, dil_wo):
    weights = (attn_norm, ffn_norm, w_gate, w_up, w_down,
               mla_wq_a, mla_q_a_norm, mla_wq_b, mla_wkv_a, mla_kv_a_norm, mla_wkv_b,
               mla_q_norm, mla_k_norm, mla_wo,
               swa_wqkv, swa_q_norm, swa_k_norm, swa_sink, swa_wo,
               dil_wqkv, dil_q_norm, dil_k_norm, dil_wo)
    return (_trunk(x_prompt, *weights), _trunk(x_sample, *weights))
```

```python
import functools
import math

import jax
import jax.numpy as jnp
from jax import lax
from jax.experimental import pallas as pl
from jax.experimental.pallas import tpu as pltpu

F32 = jnp.float32
BF16 = jnp.bfloat16

D_MODEL = 1024
DEPTH = 4
N_MIXERS = 3
ROPE_THETA = 10000.0
NORM_EPS = 1e-6
NEG_BIG = -1e30
LOG2E = math.log2(math.e)

MLA_HEADS = 8
MLA_Q_LORA = 384
MLA_KV_LORA = 256
MLA_NOPE = 128
MLA_ROPE = 64
MLA_V = 128
MLA_QK = MLA_NOPE + MLA_ROPE
MLA_QK_PAD = 256

SWA_Q_HEADS = 16
SWA_KV_HEADS = 4
SWA_HEAD_DIM = 64
SWA_HALF_WINDOW = 128

DIL_CONFIGS = ((128, 1), (512, 4), (2048, 16))
DIL_GROUPS = len(DIL_CONFIGS)
DIL_HEADS_PER_GROUP = 8
DIL_HEAD_DIM = 64
DIL_GROUP_WIDTH = DIL_HEADS_PER_GROUP * DIL_HEAD_DIM

D_FF = 2816

LANES = 128
HEAD_DIM = 64
NORM_CHUNK = 256
VMEM_LIMIT = 56 * 1024 * 1024

TM_PROJ = 512
SUB_PROJ = 256
TM_FFN = 512
SUB_FFN = 256
TQ_MLA = 256
MLA_HEADS_PER_STEP = 2
TQ_BAND = 256
PIPE_DEPTH = 3
VT_ROWS = 80
NT_DIMS = (((1,), (1,)), ((), ()))


def _rms(x, g):
    return x * lax.rsqrt(jnp.mean(x * x, axis=-1, keepdims=True) + NORM_EPS) * g


def _rope(v, c, s1, s2):
    return v * c + pltpu.roll(v, LANES - 32, 1) * s1 + pltpu.roll(v, 32, 1) * s2


def _resident(shape):
    return pl.BlockSpec(shape, lambda *_: (0,) * len(shape), pipeline_mode=pl.Buffered(1))


def _params(n_axes):
    return pltpu.CompilerParams(dimension_semantics=("parallel",) * n_axes,
                                vmem_limit_bytes=VMEM_LIMIT)


def _mla_proj_kernel(x_ref, an_ref, wqa_ref, qan_ref, wqb_ref, wkva_ref, kvan_ref, wkvb_ref,
                     gq_ref, gk_ref, c_ref, s_ref, ones_ref, q_out, k_out, v_out):
    gq, gk = gq_ref[...], gk_ref[...]
    scale = MLA_QK ** -0.5 * LOG2E
    head_w = MLA_QK_PAD + LANES

    def rms(t, g):
        n = t.shape[1]
        ms = jnp.dot((t * t).astype(BF16), ones_ref[:n, :], preferred_element_type=F32) * (1.0 / n)
        return t * jnp.concatenate([lax.rsqrt(ms + NORM_EPS)] * (n // LANES), axis=1) * g

    def project(rows):
        h = rms(x_ref[rows, :], an_ref[...]).astype(BF16)
        cq = jnp.dot(h, wqa_ref[...], preferred_element_type=F32)
        cq = rms(cq, qan_ref[...]).astype(BF16)
        qf = jnp.dot(cq, wqb_ref[...], preferred_element_type=F32)
        kva = jnp.dot(h, wkva_ref[...], preferred_element_type=F32)
        ckv = rms(kva[:, :MLA_KV_LORA], kvan_ref[...]).astype(BF16)
        kv = jnp.dot(ckv, wkvb_ref[...], preferred_element_type=F32)
        return qf, kva, kv

    def heads(rows, qf, kva, kv):
        c, s = c_ref[rows, :], s_ref[rows, :]
        kr = kva[:, MLA_KV_LORA:MLA_KV_LORA + LANES]
        kr_ss = jnp.sum(kr * kr, axis=-1, keepdims=True)
        kr_roped = kr * gk[:, LANES:2 * LANES] * c + kva[:, MLA_KV_LORA + LANES:] * gk[:, 2 * LANES:] * s
        for hh in range(MLA_HEADS):
            lo, out = hh * head_w, hh * MLA_QK_PAD
            qa, qb, qp = (qf[:, lo + i * LANES:lo + (i + 1) * LANES] for i in range(3))
            rq = lax.rsqrt(jnp.sum(qa * qa + qb * qb, axis=-1, keepdims=True) * (1.0 / MLA_QK) + NORM_EPS) * scale
            q_out[rows, out:out + LANES] = (qa * rq * gq[:, :LANES]).astype(BF16)
            q_out[rows, out + LANES:out + MLA_QK_PAD] = (
                (qb * gq[:, LANES:2 * LANES] * c + qp * gq[:, 2 * LANES:] * s) * rq).astype(BF16)
            kn = kv[:, out:out + LANES]
            rk = lax.rsqrt((jnp.sum(kn * kn, axis=-1, keepdims=True) + kr_ss) * (1.0 / MLA_QK) + NORM_EPS)
            k_out[rows, out:out + LANES] = (kn * rk * gk[:, :LANES]).astype(BF16)
            k_out[rows, out + LANES:out + MLA_QK_PAD] = (kr_roped * rk).astype(BF16)
            v_out[rows, hh * MLA_V:(hh + 1) * MLA_V] = kv[:, out + LANES:out + MLA_QK_PAD].astype(BF16)

    subs = [slice(r0, r0 + SUB_PROJ) for r0 in range(0, TM_PROJ, SUB_PROJ)]
    pending = project(subs[0])
    for si, rows in enumerate(subs):
        current = pending
        if si + 1 < len(subs):
            pending = project(subs[si + 1])
        heads(rows, *current)


def _mla_proj(x, seq, an, wqa, qan, wqb, wkva, kvan, wkvb, gq, gk, tabs):
    m = x.shape[0]
    tm = TM_PROJ
    nblk = seq // tm
    row = lambda i: (i, 0)
    pos = lambda i: (i % nblk, 0)
    hq = MLA_HEADS * MLA_QK_PAD
    in_specs = [pl.BlockSpec((tm, D_MODEL), row), _resident(an.shape), _resident(wqa.shape),
                _resident(qan.shape), _resident(wqb.shape), _resident(wkva.shape),
                _resident(kvan.shape), _resident(wkvb.shape), _resident(gq.shape), _resident(gk.shape)]
    in_specs += [pl.BlockSpec((tm, LANES), pos)] * 2
    ones = jnp.ones((D_MODEL, LANES), BF16)
    in_specs.append(_resident(ones.shape))
    return pl.pallas_call(
        _mla_proj_kernel,
        out_shape=(jax.ShapeDtypeStruct((m, hq), BF16), jax.ShapeDtypeStruct((m, hq), BF16),
                   jax.ShapeDtypeStruct((m, MLA_HEADS * MLA_V), BF16)),
        grid=(m // tm,),
        in_specs=in_specs,
        out_specs=(pl.BlockSpec((tm, hq), row), pl.BlockSpec((tm, hq), row),
                   pl.BlockSpec((tm, MLA_HEADS * MLA_V), row)),
        compiler_params=_params(1),
        name="mla_proj",
    )(x, an, wqa, qan, wqb, wkva, kvan, wkvb, gq, gk, *tabs, ones)


def _mla_attn_kernel(q_ref, k_ref, v_ref, o_ref, v1_scr, *, seq, tq):
    hps = MLA_HEADS_PER_STEP
    for h in range(hps):
        v1_scr[h, :, :MLA_V] = v_ref[0, :, h * MLA_V:(h + 1) * MLA_V]
        v1_scr[h, :, MLA_V:] = jnp.ones((seq, MLA_V), BF16)

    def scores(h, t):
        cols = slice(h * MLA_QK_PAD, (h + 1) * MLA_QK_PAD)
        return lax.dot_general(q_ref[0, t * tq:(t + 1) * tq, cols], k_ref[0, :, cols], NT_DIMS,
                               preferred_element_type=F32)

    def finish(s, h, t):
        m = jnp.max(s, axis=-1, keepdims=True)
        p = jnp.exp2(s - m).astype(BF16)
        o = jnp.dot(p, v1_scr[h], preferred_element_type=F32)
        o_ref[0, t * tq:(t + 1) * tq, h * MLA_V:(h + 1) * MLA_V] = (
            o[:, :MLA_V] / o[:, MLA_V:]).astype(o_ref.dtype)

    tiles = [(h, t) for h in range(hps) for t in range(seq // tq)]
    queue = [scores(*ht) for ht in tiles[:PIPE_DEPTH]]
    for i, ht in enumerate(tiles):
        s = queue.pop(0)
        if i + PIPE_DEPTH < len(tiles):
            queue.append(scores(*tiles[i + PIPE_DEPTH]))
        finish(s, *ht)


def _mla_attn(q, k, v):
    b, seq, _ = q.shape
    hps = MLA_HEADS_PER_STEP
    head = lambda bi, hi: (bi, 0, hi)
    return pl.pallas_call(
        functools.partial(_mla_attn_kernel, seq=seq, tq=TQ_MLA),
        out_shape=jax.ShapeDtypeStruct((b, seq, MLA_HEADS * MLA_V), BF16),
        grid=(b, MLA_HEADS // hps),
        in_specs=[pl.BlockSpec((1, seq, hps * MLA_QK_PAD), head),
                  pl.BlockSpec((1, seq, hps * MLA_QK_PAD), head),
                  pl.BlockSpec((1, seq, hps * MLA_V), head)],
        out_specs=pl.BlockSpec((1, seq, hps * MLA_V), head),
        scratch_shapes=[pltpu.VMEM((hps, seq, 2 * MLA_V), BF16)],
        compiler_params=_params(2),
        name="mla_attn",
    )(q, k, v)


def _normed_roped_chunks(qkv, col0, width, g, gmat, c, s1, s2):
    for j in range(width // NORM_CHUNK):
        t = qkv[:, col0 + j * NORM_CHUNK:col0 + (j + 1) * NORM_CHUNK]
        ms = jnp.dot((t * t).astype(BF16), gmat, preferred_element_type=F32)
        tn = t * lax.rsqrt(ms + NORM_EPS) * g
        for half in range(NORM_CHUNK // LANES):
            yield (j * NORM_CHUNK + half * LANES,
                   _rope(tn[:, half * LANES:(half + 1) * LANES], c, s1, s2))


def _projected_subtiles(x_ref, an_ref, w_ref):
    for r0 in range(0, TM_PROJ, SUB_PROJ):
        rows = slice(r0, r0 + SUB_PROJ)
        h = _rms(x_ref[rows, :], an_ref[...]).astype(BF16)
        yield rows, jnp.dot(h, w_ref[...], preferred_element_type=F32)


def _swa_proj_kernel(x_ref, an_ref, w_ref, gq_ref, gk_ref, gmat_ref, c_ref, s1_ref, s2_ref,
                     q_out, k_out, v_out, *, nq, nk):
    gmat = gmat_ref[...]
    for rows, qkv in _projected_subtiles(x_ref, an_ref, w_ref):
        tabs = (c_ref[rows, :], s1_ref[rows, :], s2_ref[rows, :])
        for lo, val in _normed_roped_chunks(qkv, 0, nq, gq_ref[...], gmat, *tabs):
            q_out[rows, lo:lo + LANES] = val.astype(BF16)
        for lo, val in _normed_roped_chunks(qkv, nq, nk, gk_ref[...], gmat, *tabs):
            k_out[rows, lo:lo + LANES] = val.astype(BF16)
        v_out[rows, :] = qkv[:, nq + nk:].astype(BF16)


def _swa_proj(x, seq, an, w, gq, gk, gmat, tabs, nq, nk):
    m = x.shape[0]
    tm = TM_PROJ
    nblk = seq // tm
    nv = w.shape[1] - nq - nk
    row = lambda i: (i, 0)
    pos = lambda i: (i % nblk, 0)
    in_specs = [pl.BlockSpec((tm, D_MODEL), row), _resident(an.shape), _resident(w.shape),
                _resident(gq.shape), _resident(gk.shape), _resident(gmat.shape)]
    in_specs += [pl.BlockSpec((tm, LANES), pos)] * 3
    return pl.pallas_call(
        functools.partial(_swa_proj_kernel, nq=nq, nk=nk),
        out_shape=(jax.ShapeDtypeStruct((m, nq), BF16), jax.ShapeDtypeStruct((m, nk), BF16),
                   jax.ShapeDtypeStruct((m, nv), BF16)),
        grid=(m // tm,),
        in_specs=in_specs,
        out_specs=(pl.BlockSpec((tm, nq), row), pl.BlockSpec((tm, nk), row), pl.BlockSpec((tm, nv), row)),
        compiler_params=_params(1),
        name="swa_proj",
    )(x, an, w, gq, gk, gmat, *tabs)


def _dil_proj_kernel(x_ref, an_ref, w_ref, gq_ref, gk_ref, gmat_ref, c_ref, s1_ref, s2_ref, *refs):
    outs, stage = refs[:3 * DIL_GROUPS], refs[3 * DIL_GROUPS]
    gmat = gmat_ref[...]
    nq = DIL_GROUPS * DIL_GROUP_WIDTH
    slabs_per_group = DIL_GROUP_WIDTH // LANES
    for rows, qkv in _projected_subtiles(x_ref, an_ref, w_ref):
        r0 = rows.start
        sub = r0 // SUB_PROJ
        tabs = (c_ref[rows, :], s1_ref[rows, :], s2_ref[rows, :])

        def emit(tensor, lo, val, sub=sub, r0=r0):
            gi, c = lo // DIL_GROUP_WIDTH, (lo % DIL_GROUP_WIDTH) // LANES
            dil = DIL_CONFIGS[gi][1]
            out = outs[3 * gi + tensor]
            cols = slice(c * LANES, (c + 1) * LANES)
            if dil == 1:
                out[0, 0, r0:r0 + SUB_PROJ, cols] = val.astype(BF16)
                return
            slot = ((sub * 3 + tensor) * (DIL_GROUPS - 1) + gi - 1) * slabs_per_group + c
            stage[slot] = val
            n = SUB_PROJ // dil
            for r in range(dil):
                out[0, r, r0 // dil:r0 // dil + n, cols] = stage[slot, pl.ds(r, n, stride=dil), :].astype(BF16)

        for lo, val in _normed_roped_chunks(qkv, 0, nq, gq_ref[...], gmat, *tabs):
            emit(0, lo, val)
        for lo, val in _normed_roped_chunks(qkv, nq, nq, gk_ref[...], gmat, *tabs):
            emit(1, lo, val)
        for lo in range(0, nq, LANES):
            emit(2, lo, qkv[:, 2 * nq + lo:2 * nq + lo + LANES])


def _dil_proj(x, b, seq, an, w, gq, gk, gmat, tabs):
    m = x.shape[0]
    tm = TM_PROJ
    nblk = seq // tm
    row = lambda i: (i, 0)
    pos = lambda i: (i % nblk, 0)
    in_specs = [pl.BlockSpec((tm, D_MODEL), row), _resident(an.shape), _resident(w.shape),
                _resident(gq.shape), _resident(gk.shape), _resident(gmat.shape)]
    in_specs += [pl.BlockSpec((tm, LANES), pos)] * 3
    out_shape, out_specs = [], []
    for _, dil in DIL_CONFIGS:
        for _ in range(3):
            out_shape.append(jax.ShapeDtypeStruct((b, dil, seq // dil, DIL_GROUP_WIDTH), BF16))
            out_specs.append(pl.BlockSpec((1, dil, tm // dil, DIL_GROUP_WIDTH),
                                          lambda i: (i // nblk, 0, i % nblk, 0)))
    n_slots = (TM_PROJ // SUB_PROJ) * 3 * (DIL_GROUPS - 1) * (DIL_GROUP_WIDTH // LANES)
    return pl.pallas_call(
        _dil_proj_kernel,
        out_shape=tuple(out_shape),
        grid=(m // tm,),
        in_specs=in_specs,
        out_specs=tuple(out_specs),
        scratch_shapes=[pltpu.VMEM((n_slots, SUB_PROJ, LANES), F32)],
        compiler_params=_params(1),
        name="dil_proj",
    )(x, an, w, gq, gk, gmat, *tabs)


def _band_bias_t(win, tq, off, half):
    dd = (lax.broadcasted_iota(jnp.int32, (win, tq), 0)
          - lax.broadcasted_iota(jnp.int32, (win, tq), 1))
    return jnp.where(jnp.abs(dd + off) <= half, 0.0, NEG_BIG).astype(F32)


def _fill_vt(vt_scr, v_block, head0, cols):
    d = HEAD_DIM
    vt = v_block.astype(F32).T.astype(BF16)
    for e in range(LANES // d):
        lo = (head0 + e) * VT_ROWS
        vt_scr[lo:lo + d, cols] = vt[e * d:(e + 1) * d, :]
        vt_scr[lo + d:lo + VT_ROWS, cols] = jnp.ones((VT_ROWS - d, vt.shape[1]), BF16)


def _probs_t(st, sk2):
    m = jnp.max(st, axis=0, keepdims=True)
    if sk2 is not None:
        m = jnp.maximum(m, sk2)
    return jnp.exp2(st - m).astype(BF16), m


def _swa_attn_kernel(q_ref, k_ref, v_ref, sink_ref, o_ref, vt_scr, *, length, tq, win, half, q_heads, group):
    d = HEAD_DIM
    chunk = 4 * LANES
    for j in range(q_heads // group * d // LANES):
        for c0 in range(0, length, chunk):
            _fill_vt(vt_scr, v_ref[0, c0:c0 + chunk, j * LANES:(j + 1) * LANES], 2 * j, slice(c0, c0 + chunk))

    def body(i, carry):
        q0 = pl.multiple_of(i * tq, tq)
        start = pl.multiple_of(jnp.clip(q0 - half, 0, length - win), LANES)
        bias_t = _band_bias_t(win, tq, start - q0, half)

        def scores(a):
            kg = k_ref[0, pl.ds(start, win), (a // group) * d:(a // group + 1) * d]
            qa = q_ref[0, pl.ds(q0, tq), a * d:(a + 1) * d]
            return lax.dot_general(kg, qa, NT_DIMS, preferred_element_type=F32) + bias_t

        queue = [scores(a) for a in range(PIPE_DEPTH)]
        pair = []
        for a in range(q_heads):
            st = queue.pop(0)
            if a + PIPE_DEPTH < q_heads:
                queue.append(scores(a + PIPE_DEPTH))
            sk2 = sink_ref[:, a:a + 1]
            pt, m = _probs_t(st, sk2)
            g = a // group
            o2t = jnp.dot(vt_scr[g * VT_ROWS:(g + 1) * VT_ROWS, pl.ds(start, win)], pt,
                          preferred_element_type=F32)
            pair.append(o2t[:d, :] / (o2t[d:d + 1, :] + jnp.exp2(sk2 - m)))
            if len(pair) == LANES // d:
                o_ref[0, pl.ds(q0, tq), (a - 1) * d:(a + 1) * d] = (
                    jnp.concatenate(pair, axis=0).T.astype(o_ref.dtype))
                pair = []
        return carry

    lax.fori_loop(0, length // tq, body, 0)


def _swa_attn(q, k, v, sink):
    b, length, nq = q.shape
    nk = k.shape[-1]
    tq, half = TQ_BAND, SWA_HALF_WINDOW
    whole = lambda bi: (bi, 0, 0)
    return pl.pallas_call(
        functools.partial(_swa_attn_kernel, length=length, tq=tq, win=tq + 2 * half, half=half,
                          q_heads=SWA_Q_HEADS, group=SWA_Q_HEADS // SWA_KV_HEADS),
        out_shape=jax.ShapeDtypeStruct((b, length, nq), BF16),
        grid=(b,),
        in_specs=[pl.BlockSpec((1, length, nq), whole), pl.BlockSpec((1, length, nk), whole),
                  pl.BlockSpec((1, length, nk), whole), _resident(sink.shape)],
        out_specs=pl.BlockSpec((1, length, nq), whole),
        scratch_shapes=[pltpu.VMEM((SWA_KV_HEADS * VT_ROWS, length), BF16)],
        compiler_params=_params(1),
        name="swa_attn",
    )(q, k, v, sink)


def _dil_attn_kernel(*refs, seq, tq):
    qkv, o_ref = refs[:3 * DIL_GROUPS], refs[3 * DIL_GROUPS]
    acc_scr, lse_scr, vt_scr = refs[3 * DIL_GROUPS + 1:]
    d, heads = DIL_HEAD_DIM, DIL_HEADS_PER_GROUP
    slabs = DIL_GROUP_WIDTH // LANES

    for gi, (window, dil) in enumerate(DIL_CONFIGS):
        q_ref, k_ref, v_ref = qkv[3 * gi:3 * gi + 3]
        length = seq // dil
        half = window // (2 * dil)
        t_q = min(tq, length)
        win = min(length, t_q + 2 * LANES)
        n_tiles = length // t_q

        def fill(r, carry, v_ref=v_ref, length=length):
            cols = pl.ds(pl.multiple_of(r * length, LANES), length)
            for j in range(slabs):
                _fill_vt(vt_scr, v_ref[0, r, :, j * LANES:(j + 1) * LANES], 2 * j, cols)
            return carry

        lax.fori_loop(0, dil, fill, 0)

        def tile(idx, carry, gi=gi, dil=dil, length=length, half=half, t_q=t_q, win=win,
                 n_tiles=n_tiles, q_ref=q_ref, k_ref=k_ref):
            r = idx // n_tiles
            q0 = pl.multiple_of((idx % n_tiles) * t_q, t_q)
            start = pl.multiple_of(jnp.clip(q0 - LANES, 0, length - win), LANES)
            bias_t = _band_bias_t(win, t_q, start - q0, half)
            rows = pl.ds(q0 * dil + r, t_q, stride=dil) if dil > 1 else pl.ds(q0, t_q)
            vcols = pl.ds(pl.multiple_of(r * length + start, LANES), win)

            def scores(hh):
                qa = q_ref[0, r, pl.ds(q0, t_q), hh * d:(hh + 1) * d]
                kg = k_ref[0, r, pl.ds(start, win), hh * d:(hh + 1) * d]
                return lax.dot_general(kg, qa, NT_DIMS, preferred_element_type=F32) + bias_t

            def merge(c, o_new, l_new):
                if gi == 0:
                    acc_scr[c, rows, :] = o_new
                    lse_scr[c, rows, :] = l_new
                    return
                l_old = lse_scr[c, rows, :]
                mx = jnp.maximum(l_old, l_new)
                wa = jnp.exp2(l_old - mx)
                wb = jnp.exp2(l_new - mx)
                acc_scr[c, rows, :] = (wa * acc_scr[c, rows, :] + wb * o_new) / (wa + wb)
                if gi + 1 < DIL_GROUPS:
                    lse_scr[c, rows, :] = mx + jnp.log2(wa + wb)

            queue = [scores(hh) for hh in range(PIPE_DEPTH)]
            o_pair, l_pair = [], []
            for hh in range(heads):
                st = queue.pop(0)
                if hh + PIPE_DEPTH < heads:
                    queue.append(scores(hh + PIPE_DEPTH))
                pt, m = _probs_t(st, None)
                o2t = jnp.dot(vt_scr[hh * VT_ROWS:(hh + 1) * VT_ROWS, vcols], pt,
                              preferred_element_type=F32)
                den = o2t[d:d + 1, :]
                o_pair.append(o2t[:d, :] / den)
                l_pair.append(jnp.broadcast_to(m + jnp.log2(den), (d, t_q)))
                if len(o_pair) == LANES // d:
                    merge(hh // 2, jnp.concatenate(o_pair, axis=0).T, jnp.concatenate(l_pair, axis=0).T)
                    o_pair, l_pair = [], []
            return carry

        lax.fori_loop(0, dil * n_tiles, tile, 0)

    for c in range(slabs):
        o_ref[0, :, c * LANES:(c + 1) * LANES] = acc_scr[c].astype(o_ref.dtype)


def _dil_attn(qkv9, seq):
    b = qkv9[0].shape[0]
    in_specs = [pl.BlockSpec((1,) + t.shape[1:], lambda bi: (bi, 0, 0, 0)) for t in qkv9]
    slabs = DIL_GROUP_WIDTH // LANES
    return pl.pallas_call(
        functools.partial(_dil_attn_kernel, seq=seq, tq=TQ_BAND),
        out_shape=jax.ShapeDtypeStruct((b, seq, DIL_GROUP_WIDTH), BF16),
        grid=(b,),
        in_specs=in_specs,
        out_specs=pl.BlockSpec((1, seq, DIL_GROUP_WIDTH), lambda bi: (bi, 0, 0)),
        scratch_shapes=[pltpu.VMEM((slabs, seq, LANES), F32), pltpu.VMEM((slabs, seq, LANES), F32),
                        pltpu.VMEM((DIL_HEADS_PER_GROUP * VT_ROWS, seq), BF16)],
        compiler_params=_params(1),
        name="dil_attn",
    )(*qkv9)


def _out_ffn_kernel(x_ref, o_ref, wo_ref, fn_ref, wg_ref, wu_ref, wd_ref, y_ref):
    subs = [slice(r0, r0 + SUB_FFN) for r0 in range(0, TM_FFN, SUB_FFN)]
    x1s = [x_ref[rows, :] + jnp.dot(o_ref[rows, :], wo_ref[...], preferred_element_type=F32) for rows in subs]
    hs = [_rms(x1, fn_ref[...]).astype(BF16) for x1 in x1s]
    gate_up = [(jnp.dot(h, wg_ref[...], preferred_element_type=F32),
                jnp.dot(h, wu_ref[...], preferred_element_type=F32)) for h in hs]
    acts = [(gate * jax.nn.sigmoid(gate) * up).astype(BF16) for gate, up in gate_up]
    for rows, x1, act in zip(subs, x1s, acts):
        y_ref[rows, :] = x1 + jnp.dot(act, wd_ref[...], preferred_element_type=F32)


def _out_ffn(x, o, wo, fn, wg, wu, wd):
    m = x.shape[0]
    tm = TM_FFN
    row = lambda i: (i, 0)
    in_specs = [pl.BlockSpec((tm, D_MODEL), row), pl.BlockSpec((tm, o.shape[1]), row),
                _resident(wo.shape), _resident(fn.shape), _resident(wg.shape), _resident(wu.shape),
                _resident(wd.shape)]
    return pl.pallas_call(
        _out_ffn_kernel,
        out_shape=jax.ShapeDtypeStruct((m, D_MODEL), F32),
        grid=(m // tm,),
        in_specs=in_specs,
        out_specs=pl.BlockSpec((tm, D_MODEL), row),
        compiler_params=_params(1),
        name="out_ffn",
    )(x, o, wo, fn, wg, wu, wd)


def _rope_tables(seq, dim, lanes):
    inv = 1.0 / (ROPE_THETA ** (jnp.arange(0, dim, 2, dtype=F32) / dim))
    ang = jnp.arange(seq, dtype=F32)[:, None] * inv[None, :]
    cos, sin = jnp.cos(ang), jnp.sin(ang)
    zero = jnp.zeros_like(sin)
    tabs = (jnp.concatenate([cos, cos], -1), jnp.concatenate([-sin, zero], -1),
            jnp.concatenate([zero, sin], -1))
    return tuple(jnp.tile(t, (1, lanes // dim)) for t in tabs)


def _row(v):
    return v.reshape(1, -1).astype(F32)


def _with_rope_partner(t):
    half = MLA_ROPE // 2
    zeros = jnp.zeros(t.shape[:-1] + (LANES - MLA_ROPE,), t.dtype)
    partner = jnp.concatenate([t[..., half:], t[..., :half]], -1)
    return jnp.concatenate([t, zeros, partner, zeros], -1)


def _head_gain(g, reps, scale=1.0):
    return jnp.tile(g.astype(F32) * scale, reps).reshape(1, -1)


def _trunk(x, attn_norm, ffn_norm, w_gate, w_up, w_down,
           mla_wq_a, mla_q_a_norm, mla_wq_b, mla_wkv_a, mla_kv_a_norm, mla_wkv_b,
           mla_q_norm, mla_k_norm, mla_wo,
           swa_wqkv, swa_q_norm, swa_k_norm, swa_sink, swa_wo,
           dil_wqkv, dil_q_norm, dil_k_norm, dil_wo):
    b, seq, _ = x.shape
    m = b * seq
    xf = x.reshape(m, D_MODEL)
    tabs64 = _rope_tables(seq, HEAD_DIM, LANES)
    cos, msin, psin = _rope_tables(seq, MLA_ROPE, MLA_ROPE)
    pad = ((0, 0), (0, LANES - MLA_ROPE))
    tabs_mla = (jnp.pad(cos, pad), jnp.pad(msin + psin, pad))
    blk = jnp.arange(NORM_CHUNK) // HEAD_DIM
    gmat = ((blk[:, None] == blk[None, :]).astype(F32) / HEAD_DIM).astype(BF16)
    reps = NORM_CHUNK // HEAD_DIM

    for i in range(DEPTH):
        kind, j = i % N_MIXERS, i // N_MIXERS
        an = _row(attn_norm[i])
        if kind == 0:
            wqb = mla_wq_b[j].reshape(MLA_Q_LORA, MLA_HEADS, MLA_QK)
            wqb = jnp.concatenate([wqb[..., :MLA_NOPE], _with_rope_partner(wqb[..., MLA_NOPE:])], -1)
            wkva = jnp.concatenate([mla_wkv_a[j][:, :MLA_KV_LORA],
                                    _with_rope_partner(mla_wkv_a[j][:, MLA_KV_LORA:])], -1)
            gains = [jnp.concatenate([g[:MLA_NOPE], _with_rope_partner(g[MLA_NOPE:])])
                     for g in (mla_q_norm[j], mla_k_norm[j])]
            q, k, v = _mla_proj(
                xf, seq, an, mla_wq_a[j].astype(BF16), _row(mla_q_a_norm[j]),
                wqb.reshape(MLA_Q_LORA, -1).astype(BF16), wkva.astype(BF16), _row(mla_kv_a_norm[j]),
                mla_wkv_b[j].astype(BF16), _row(gains[0]), _row(gains[1]), tabs_mla)
            o = _mla_attn(q.reshape(b, seq, -1), k.reshape(b, seq, -1), v.reshape(b, seq, -1))
            wo = mla_wo[j]
        elif kind == 1:
            nq, nk = SWA_Q_HEADS * SWA_HEAD_DIM, SWA_KV_HEADS * SWA_HEAD_DIM
            q, k, v = _swa_proj(xf, seq, an, swa_wqkv[j].astype(BF16),
                                _head_gain(swa_q_norm[j], reps, SWA_HEAD_DIM ** -0.5 * LOG2E),
                                _head_gain(swa_k_norm[j], reps), gmat, tabs64, nq, nk)
            o = _swa_attn(q.reshape(b, seq, nq), k.reshape(b, seq, nk), v.reshape(b, seq, nk),
                          _row(swa_sink[j]) * LOG2E)
            wo = swa_wo[j]
        else:
            qkv9 = _dil_proj(xf, b, seq, an, dil_wqkv[j].astype(BF16),
                             _head_gain(dil_q_norm[j], reps, DIL_HEAD_DIM ** -0.5 * LOG2E),
                             _head_gain(dil_k_norm[j], reps), gmat, tabs64)
            o = _dil_attn(qkv9, seq)
            wo = dil_wo[j]
        xf = _out_ffn(xf, o.reshape(m, -1), wo.astype(BF16), _row(ffn_norm[i]), w_gate[i].astype(BF16),
                      w_up[i].astype(BF16), w_down[i].astype(BF16))
    return xf.reshape(b, seq, D_MODEL)


def kernel(x_prompt, x_sample, attn_norm, ffn_norm, w_gate, w_up, w_down, mla_wq_a, mla_q_a_norm, mla_wq_b, mla_wkv_a, mla_kv_a_norm, mla_wkv_b, mla_q_norm, mla_k_norm, mla_wo, swa_wqkv, swa_q_norm, swa_k_norm, swa_sink, swa_wo, dil_wqkv, dil_q_norm, dil_k_norm, dil_wo):
    weights = (attn_norm, ffn_norm, w_gate, w_up, w_down,
               mla_wq_a, mla_q_a_norm, mla_wq_b, mla_wkv_a, mla_kv_a_norm, mla_wkv_b,
               mla_q_norm, mla_k_norm, mla_wo,
               swa_wqkv, swa_q_norm, swa_k_norm, swa_sink, swa_wo,
               dil_wqkv, dil_q_norm, dil_k_norm, dil_wo)
    return (_trunk(x_prompt, *weights), _trunk(x_sample, *weights))
```

```python
import functools
import math

import jax
import jax.numpy as jnp
from jax import lax
from jax.experimental import pallas as pl
from jax.experimental.pallas import tpu as pltpu

F32 = jnp.float32
BF16 = jnp.bfloat16

D_MODEL = 1024
DEPTH = 4
N_MIXERS = 3
ROPE_THETA = 10000.0
NORM_EPS = 1e-6
NEG_BIG = -1e30
LOG2E = math.log2(math.e)

MLA_HEADS = 8
MLA_Q_LORA = 384
MLA_KV_LORA = 256
MLA_NOPE = 128
MLA_ROPE = 64
MLA_V = 128
MLA_QK = MLA_NOPE + MLA_ROPE
MLA_QK_PAD = 256

SWA_Q_HEADS = 16
SWA_KV_HEADS = 4
SWA_HEAD_DIM = 64
SWA_HALF_WINDOW = 128

DIL_CONFIGS = ((128, 1), (512, 4), (2048, 16))
DIL_GROUPS = len(DIL_CONFIGS)
DIL_HEADS_PER_GROUP = 8
DIL_HEAD_DIM = 64
DIL_GROUP_WIDTH = DIL_HEADS_PER_GROUP * DIL_HEAD_DIM

D_FF = 2816

LANES = 128
HEAD_DIM = 64
NORM_CHUNK = 256
VMEM_LIMIT = 56 * 1024 * 1024

TM_PROJ = 512
SUB_PROJ = 256
TM_FFN = 512
SUB_FFN = 256
TQ_MLA = 256
MLA_HEADS_PER_STEP = 2
TQ_BAND = 256
PIPE_DEPTH = 3
PROJ_DEPTH = 2
VT_ROWS = 80
NT_DIMS = (((1,), (1,)), ((), ()))


def _rms(x, g):
    return x * lax.rsqrt(jnp.mean(x * x, axis=-1, keepdims=True) + NORM_EPS) * g


def _rope(v, c, s1, s2):
    return v * c + pltpu.roll(v, LANES - 32, 1) * s1 + pltpu.roll(v, 32, 1) * s2


def _resident(shape):
    return pl.BlockSpec(shape, lambda *_: (0,) * len(shape), pipeline_mode=pl.Buffered(1))


def _params(n_axes):
    return pltpu.CompilerParams(dimension_semantics=("parallel",) * n_axes,
                                vmem_limit_bytes=VMEM_LIMIT)


def _mla_proj_kernel(x_ref, an_ref, wqa_ref, qan_ref, wqb_ref, wkva_ref, kvan_ref, wkvb_ref,
                     gq_ref, gk_ref, c_ref, s_ref, ones_ref, q_out, k_out, v_out):
    gq, gk = gq_ref[...], gk_ref[...]
    scale = MLA_QK ** -0.5 * LOG2E
    head_w = MLA_QK_PAD + LANES

    def rms(t, g):
        n = t.shape[1]
        ms = jnp.dot((t * t).astype(BF16), ones_ref[:n, :], preferred_element_type=F32) * (1.0 / n)
        return t * jnp.concatenate([lax.rsqrt(ms + NORM_EPS)] * (n // LANES), axis=1) * g

    def project(rows):
        h = rms(x_ref[rows, :], an_ref[...]).astype(BF16)
        cq = jnp.dot(h, wqa_ref[...], preferred_element_type=F32)
        cq = rms(cq, qan_ref[...]).astype(BF16)
        qf = jnp.dot(cq, wqb_ref[...], preferred_element_type=F32)
        kva = jnp.dot(h, wkva_ref[...], preferred_element_type=F32)
        ckv = rms(kva[:, :MLA_KV_LORA], kvan_ref[...]).astype(BF16)
        kv = jnp.dot(ckv, wkvb_ref[...], preferred_element_type=F32)
        return qf, kva, kv

    def heads(rows, qf, kva, kv):
        c, s = c_ref[rows, :], s_ref[rows, :]
        kr = kva[:, MLA_KV_LORA:MLA_KV_LORA + LANES]
        kr_ss = jnp.sum(kr * kr, axis=-1, keepdims=True)
        kr_roped = kr * gk[:, LANES:2 * LANES] * c + kva[:, MLA_KV_LORA + LANES:] * gk[:, 2 * LANES:] * s
        for hh in range(MLA_HEADS):
            lo, out = hh * head_w, hh * MLA_QK_PAD
            qa, qb, qp = (qf[:, lo + i * LANES:lo + (i + 1) * LANES] for i in range(3))
            rq = lax.rsqrt(jnp.sum(qa * qa + qb * qb, axis=-1, keepdims=True) * (1.0 / MLA_QK) + NORM_EPS) * scale
            q_out[rows, out:out + LANES] = (qa * rq * gq[:, :LANES]).astype(BF16)
            q_out[rows, out + LANES:out + MLA_QK_PAD] = (
                (qb * gq[:, LANES:2 * LANES] * c + qp * gq[:, 2 * LANES:] * s) * rq).astype(BF16)
            kn = kv[:, out:out + LANES]
            rk = lax.rsqrt((jnp.sum(kn * kn, axis=-1, keepdims=True) + kr_ss) * (1.0 / MLA_QK) + NORM_EPS)
            k_out[rows, out:out + LANES] = (kn * rk * gk[:, :LANES]).astype(BF16)
            k_out[rows, out + LANES:out + MLA_QK_PAD] = (kr_roped * rk).astype(BF16)
            v_out[rows, hh * MLA_V:(hh + 1) * MLA_V] = kv[:, out + LANES:out + MLA_QK_PAD].astype(BF16)

    subs = [slice(r0, r0 + SUB_PROJ) for r0 in range(0, TM_PROJ, SUB_PROJ)]
    pending = project(subs[0])
    for si, rows in enumerate(subs):
        current = pending
        if si + 1 < len(subs):
            pending = project(subs[si + 1])
        heads(rows, *current)


def _mla_proj(x, seq, an, wqa, qan, wqb, wkva, kvan, wkvb, gq, gk, tabs):
    m = x.shape[0]
    tm = TM_PROJ
    nblk = seq // tm
    row = lambda i: (i, 0)
    pos = lambda i: (i % nblk, 0)
    hq = MLA_HEADS * MLA_QK_PAD
    in_specs = [pl.BlockSpec((tm, D_MODEL), row), _resident(an.shape), _resident(wqa.shape),
                _resident(qan.shape), _resident(wqb.shape), _resident(wkva.shape),
                _resident(kvan.shape), _resident(wkvb.shape), _resident(gq.shape), _resident(gk.shape)]
    in_specs += [pl.BlockSpec((tm, LANES), pos)] * 2
    ones = jnp.ones((D_MODEL, LANES), BF16)
    in_specs.append(_resident(ones.shape))
    return pl.pallas_call(
        _mla_proj_kernel,
        out_shape=(jax.ShapeDtypeStruct((m, hq), BF16), jax.ShapeDtypeStruct((m, hq), BF16),
                   jax.ShapeDtypeStruct((m, MLA_HEADS * MLA_V), BF16)),
        grid=(m // tm,),
        in_specs=in_specs,
        out_specs=(pl.BlockSpec((tm, hq), row), pl.BlockSpec((tm, hq), row),
                   pl.BlockSpec((tm, MLA_HEADS * MLA_V), row)),
        compiler_params=_params(1),
        name="mla_proj",
    )(x, an, wqa, qan, wqb, wkva, kvan, wkvb, gq, gk, *tabs, ones)


def _mla_attn_kernel(q_ref, k_ref, v_ref, o_ref, v1_scr, *, seq, tq):
    hps = MLA_HEADS_PER_STEP
    for h in range(hps):
        v1_scr[h, :, :MLA_V] = v_ref[0, :, h * MLA_V:(h + 1) * MLA_V]
        v1_scr[h, :, MLA_V:] = jnp.ones((seq, MLA_V), BF16)

    def scores(h, t):
        cols = slice(h * MLA_QK_PAD, (h + 1) * MLA_QK_PAD)
        return lax.dot_general(q_ref[0, t * tq:(t + 1) * tq, cols], k_ref[0, :, cols], NT_DIMS,
                               preferred_element_type=F32)

    def finish(s, h, t):
        m = jnp.max(s, axis=-1, keepdims=True)
        p = jnp.exp2(s - m).astype(BF16)
        o = jnp.dot(p, v1_scr[h], preferred_element_type=F32)
        o_ref[0, t * tq:(t + 1) * tq, h * MLA_V:(h + 1) * MLA_V] = (
            o[:, :MLA_V] / o[:, MLA_V:]).astype(o_ref.dtype)

    tiles = [(h, t) for h in range(hps) for t in range(seq // tq)]
    queue = [scores(*ht) for ht in tiles[:PIPE_DEPTH]]
    for i, ht in enumerate(tiles):
        s = queue.pop(0)
        if i + PIPE_DEPTH < len(tiles):
            queue.append(scores(*tiles[i + PIPE_DEPTH]))
        finish(s, *ht)


def _mla_attn(q, k, v):
    b, seq, _ = q.shape
    hps = MLA_HEADS_PER_STEP
    head = lambda bi, hi: (bi, 0, hi)
    return pl.pallas_call(
        functools.partial(_mla_attn_kernel, seq=seq, tq=TQ_MLA),
        out_shape=jax.ShapeDtypeStruct((b, seq, MLA_HEADS * MLA_V), BF16),
        grid=(b, MLA_HEADS // hps),
        in_specs=[pl.BlockSpec((1, seq, hps * MLA_QK_PAD), head),
                  pl.BlockSpec((1, seq, hps * MLA_QK_PAD), head),
                  pl.BlockSpec((1, seq, hps * MLA_V), head)],
        out_specs=pl.BlockSpec((1, seq, hps * MLA_V), head),
        scratch_shapes=[pltpu.VMEM((hps, seq, 2 * MLA_V), BF16)],
        compiler_params=_params(2),
        name="mla_attn",
    )(q, k, v)


def _projected_chunks(x_ref, an_ref, w_ref):
    offsets = list(range(0, w_ref.shape[1], NORM_CHUNK))
    for r0 in range(0, TM_PROJ, SUB_PROJ):
        rows = slice(r0, r0 + SUB_PROJ)
        h = _rms(x_ref[rows, :], an_ref[...]).astype(BF16)
        chunk = lambda c0: jnp.dot(h, w_ref[:, c0:c0 + NORM_CHUNK], preferred_element_type=F32)
        queue = [chunk(c0) for c0 in offsets[:PROJ_DEPTH]]
        for i, c0 in enumerate(offsets):
            t = queue.pop(0)
            if i + PROJ_DEPTH < len(offsets):
                queue.append(chunk(offsets[i + PROJ_DEPTH]))
            yield rows, c0, t


def _normed_roped_halves(t, g, gmat, c, s1, s2):
    ms = jnp.dot((t * t).astype(BF16), gmat, preferred_element_type=F32)
    tn = t * lax.rsqrt(ms + NORM_EPS) * g
    for half in range(NORM_CHUNK // LANES):
        yield half * LANES, _rope(tn[:, half * LANES:(half + 1) * LANES], c, s1, s2)


def _swa_proj_kernel(x_ref, an_ref, w_ref, gq_ref, gk_ref, gmat_ref, c_ref, s1_ref, s2_ref,
                     q_out, k_out, v_out, *, nq, nk):
    gmat = gmat_ref[...]
    for rows, c0, t in _projected_chunks(x_ref, an_ref, w_ref):
        if c0 >= nq + nk:
            v_out[rows, c0 - nq - nk:c0 - nq - nk + NORM_CHUNK] = t.astype(BF16)
            continue
        out, g, lo = (q_out, gq_ref, c0) if c0 < nq else (k_out, gk_ref, c0 - nq)
        tabs = (c_ref[rows, :], s1_ref[rows, :], s2_ref[rows, :])
        for off, val in _normed_roped_halves(t, g[...], gmat, *tabs):
            out[rows, lo + off:lo + off + LANES] = val.astype(BF16)


def _swa_proj(x, seq, an, w, gq, gk, gmat, tabs, nq, nk):
    m = x.shape[0]
    tm = TM_PROJ
    nblk = seq // tm
    nv = w.shape[1] - nq - nk
    row = lambda i: (i, 0)
    pos = lambda i: (i % nblk, 0)
    in_specs = [pl.BlockSpec((tm, D_MODEL), row), _resident(an.shape), _resident(w.shape),
                _resident(gq.shape), _resident(gk.shape), _resident(gmat.shape)]
    in_specs += [pl.BlockSpec((tm, LANES), pos)] * 3
    return pl.pallas_call(
        functools.partial(_swa_proj_kernel, nq=nq, nk=nk),
        out_shape=(jax.ShapeDtypeStruct((m, nq), BF16), jax.ShapeDtypeStruct((m, nk), BF16),
                   jax.ShapeDtypeStruct((m, nv), BF16)),
        grid=(m // tm,),
        in_specs=in_specs,
        out_specs=(pl.BlockSpec((tm, nq), row), pl.BlockSpec((tm, nk), row), pl.BlockSpec((tm, nv), row)),
        compiler_params=_params(1),
        name="swa_proj",
    )(x, an, w, gq, gk, gmat, *tabs)


def _dil_proj_kernel(x_ref, an_ref, w_ref, gq_ref, gk_ref, gmat_ref, c_ref, s1_ref, s2_ref, *refs):
    outs, stage = refs[:3 * DIL_GROUPS], refs[3 * DIL_GROUPS]
    gmat = gmat_ref[...]
    nq = DIL_GROUPS * DIL_GROUP_WIDTH
    slabs_per_group = DIL_GROUP_WIDTH // LANES
    for rows, c0, t in _projected_chunks(x_ref, an_ref, w_ref):
        r0 = rows.start
        sub = r0 // SUB_PROJ
        tensor, lo0 = c0 // nq, c0 % nq

        def emit(tensor, lo, val, sub=sub, r0=r0):
            gi, c = lo // DIL_GROUP_WIDTH, (lo % DIL_GROUP_WIDTH) // LANES
            dil = DIL_CONFIGS[gi][1]
            out = outs[3 * gi + tensor]
            cols = slice(c * LANES, (c + 1) * LANES)
            if dil == 1:
                out[0, 0, r0:r0 + SUB_PROJ, cols] = val.astype(BF16)
                return
            slot = ((sub * 3 + tensor) * (DIL_GROUPS - 1) + gi - 1) * slabs_per_group + c
            stage[slot] = val
            n = SUB_PROJ // dil
            for r in range(dil):
                out[0, r, r0 // dil:r0 // dil + n, cols] = stage[slot, pl.ds(r, n, stride=dil), :].astype(BF16)

        if tensor == 2:
            for off in range(0, NORM_CHUNK, LANES):
                emit(2, lo0 + off, t[:, off:off + LANES])
            continue
        tabs = (c_ref[rows, :], s1_ref[rows, :], s2_ref[rows, :])
        for off, val in _normed_roped_halves(t, (gq_ref, gk_ref)[tensor][...], gmat, *tabs):
            emit(tensor, lo0 + off, val)


def _dil_proj(x, b, seq, an, w, gq, gk, gmat, tabs):
    m = x.shape[0]
    tm = TM_PROJ
    nblk = seq // tm
    row = lambda i: (i, 0)
    pos = lambda i: (i % nblk, 0)
    in_specs = [pl.BlockSpec((tm, D_MODEL), row), _resident(an.shape), _resident(w.shape),
                _resident(gq.shape), _resident(gk.shape), _resident(gmat.shape)]
    in_specs += [pl.BlockSpec((tm, LANES), pos)] * 3
    out_shape, out_specs = [], []
    for _, dil in DIL_CONFIGS:
        for _ in range(3):
            out_shape.append(jax.ShapeDtypeStruct((b, dil, seq // dil, DIL_GROUP_WIDTH), BF16))
            out_specs.append(pl.BlockSpec((1, dil, tm // dil, DIL_GROUP_WIDTH),
                                          lambda i: (i // nblk, 0, i % nblk, 0)))
    n_slots = (TM_PROJ // SUB_PROJ) * 3 * (DIL_GROUPS - 1) * (DIL_GROUP_WIDTH // LANES)
    return pl.pallas_call(
        _dil_proj_kernel,
        out_shape=tuple(out_shape),
        grid=(m // tm,),
        in_specs=in_specs,
        out_specs=tuple(out_specs),
        scratch_shapes=[pltpu.VMEM((n_slots, SUB_PROJ, LANES), F32)],
        compiler_params=_params(1),
        name="dil_proj",
    )(x, an, w, gq, gk, gmat, *tabs)


def _band_bias_t(win, tq, off, half):
    dd = (lax.broadcasted_iota(jnp.int32, (win, tq), 0)
          - lax.broadcasted_iota(jnp.int32, (win, tq), 1))
    return jnp.where(jnp.abs(dd + off) <= half, 0.0, NEG_BIG).astype(F32)


def _fill_vt(vt_scr, v_block, head0, cols):
    d = HEAD_DIM
    vt = v_block.astype(F32).T.astype(BF16)
    for e in range(LANES // d):
        lo = (head0 + e) * VT_ROWS
        vt_scr[lo:lo + d, cols] = vt[e * d:(e + 1) * d, :]
        vt_scr[lo + d:lo + VT_ROWS, cols] = jnp.ones((VT_ROWS - d, vt.shape[1]), BF16)


def _probs_t(st, sk2):
    m = jnp.max(st, axis=0, keepdims=True)
    if sk2 is not None:
        m = jnp.maximum(m, sk2)
    return jnp.exp2(st - m).astype(BF16), m


def _swa_attn_kernel(q_ref, k_ref, v_ref, sink_ref, o_ref, vt_scr, *, length, tq, win, half, q_heads, group):
    d = HEAD_DIM
    chunk = 4 * LANES
    for j in range(q_heads // group * d // LANES):
        for c0 in range(0, length, chunk):
            _fill_vt(vt_scr, v_ref[0, c0:c0 + chunk, j * LANES:(j + 1) * LANES], 2 * j, slice(c0, c0 + chunk))

    def body(i, carry):
        q0 = pl.multiple_of(i * tq, tq)
        start = pl.multiple_of(jnp.clip(q0 - half, 0, length - win), LANES)
        bias_t = _band_bias_t(win, tq, start - q0, half)

        def scores(a):
            kg = k_ref[0, pl.ds(start, win), (a // group) * d:(a // group + 1) * d]
            qa = q_ref[0, pl.ds(q0, tq), a * d:(a + 1) * d]
            return lax.dot_general(kg, qa, NT_DIMS, preferred_element_type=F32) + bias_t

        queue = [scores(a) for a in range(PIPE_DEPTH)]
        pair = []
        for a in range(q_heads):
            st = queue.pop(0)
            if a + PIPE_DEPTH < q_heads:
                queue.append(scores(a + PIPE_DEPTH))
            sk2 = sink_ref[:, a:a + 1]
            pt, m = _probs_t(st, sk2)
            g = a // group
            o2t = jnp.dot(vt_scr[g * VT_ROWS:(g + 1) * VT_ROWS, pl.ds(start, win)], pt,
                          preferred_element_type=F32)
            pair.append(o2t[:d, :] / (o2t[d:d + 1, :] + jnp.exp2(sk2 - m)))
            if len(pair) == LANES // d:
                o_ref[0, pl.ds(q0, tq), (a - 1) * d:(a + 1) * d] = (
                    jnp.concatenate(pair, axis=0).T.astype(o_ref.dtype))
                pair = []
        return carry

    lax.fori_loop(0, length // tq, body, 0)


def _swa_attn(q, k, v, sink):
    b, length, nq = q.shape
    nk = k.shape[-1]
    tq, half = TQ_BAND, SWA_HALF_WINDOW
    whole = lambda bi: (bi, 0, 0)
    return pl.pallas_call(
        functools.partial(_swa_attn_kernel, length=length, tq=tq, win=tq + 2 * half, half=half,
                          q_heads=SWA_Q_HEADS, group=SWA_Q_HEADS // SWA_KV_HEADS),
        out_shape=jax.ShapeDtypeStruct((b, length, nq), BF16),
        grid=(b,),
        in_specs=[pl.BlockSpec((1, length, nq), whole), pl.BlockSpec((1, length, nk), whole),
                  pl.BlockSpec((1, length, nk), whole), _resident(sink.shape)],
        out_specs=pl.BlockSpec((1, length, nq), whole),
        scratch_shapes=[pltpu.VMEM((SWA_KV_HEADS * VT_ROWS, length), BF16)],
        compiler_params=_params(1),
        name="swa_attn",
    )(q, k, v, sink)


def _dil_attn_kernel(*refs, seq, tq):
    qkv, o_ref = refs[:3 * DIL_GROUPS], refs[3 * DIL_GROUPS]
    acc_scr, lse_scr, vt_scr = refs[3 * DIL_GROUPS + 1:]
    d, heads = DIL_HEAD_DIM, DIL_HEADS_PER_GROUP
    slabs = DIL_GROUP_WIDTH // LANES

    for gi, (window, dil) in enumerate(DIL_CONFIGS):
        q_ref, k_ref, v_ref = qkv[3 * gi:3 * gi + 3]
        length = seq // dil
        half = window // (2 * dil)
        t_q = min(tq, length)
        win = min(length, t_q + 2 * LANES)
        n_tiles = length // t_q

        def fill(r, carry, v_ref=v_ref, length=length):
            cols = pl.ds(pl.multiple_of(r * length, LANES), length)
            for j in range(slabs):
                _fill_vt(vt_scr, v_ref[0, r, :, j * LANES:(j + 1) * LANES], 2 * j, cols)
            return carry

        lax.fori_loop(0, dil, fill, 0)

        def tile(idx, carry, gi=gi, dil=dil, length=length, half=half, t_q=t_q, win=win,
                 n_tiles=n_tiles, q_ref=q_ref, k_ref=k_ref):
            r = idx // n_tiles
            q0 = pl.multiple_of((idx % n_tiles) * t_q, t_q)
            start = pl.multiple_of(jnp.clip(q0 - LANES, 0, length - win), LANES)
            bias_t = _band_bias_t(win, t_q, start - q0, half)
            rows = pl.ds(q0 * dil + r, t_q, stride=dil) if dil > 1 else pl.ds(q0, t_q)
            vcols = pl.ds(pl.multiple_of(r * length + start, LANES), win)

            def scores(hh):
                qa = q_ref[0, r, pl.ds(q0, t_q), hh * d:(hh + 1) * d]
                kg = k_ref[0, r, pl.ds(start, win), hh * d:(hh + 1) * d]
                return lax.dot_general(kg, qa, NT_DIMS, preferred_element_type=F32) + bias_t

            def merge(c, o_new, l_new):
                if gi == 0:
                    acc_scr[c, rows, :] = o_new
                    lse_scr[c, rows, :] = l_new
                    return
                l_old = lse_scr[c, rows, :]
                mx = jnp.maximum(l_old, l_new)
                wa = jnp.exp2(l_old - mx)
                wb = jnp.exp2(l_new - mx)
                acc_scr[c, rows, :] = (wa * acc_scr[c, rows, :] + wb * o_new) / (wa + wb)
                if gi + 1 < DIL_GROUPS:
                    lse_scr[c, rows, :] = mx + jnp.log2(wa + wb)

            queue = [scores(hh) for hh in range(PIPE_DEPTH)]
            o_pair, l_pair = [], []
            for hh in range(heads):
                st = queue.pop(0)
                if hh + PIPE_DEPTH < heads:
                    queue.append(scores(hh + PIPE_DEPTH))
                pt, m = _probs_t(st, None)
                o2t = jnp.dot(vt_scr[hh * VT_ROWS:(hh + 1) * VT_ROWS, vcols], pt,
                              preferred_element_type=F32)
                den = o2t[d:d + 1, :]
                o_pair.append(o2t[:d, :] / den)
                l_pair.append(jnp.broadcast_to(m + jnp.log2(den), (d, t_q)))
                if len(o_pair) == LANES // d:
                    merge(hh // 2, jnp.concatenate(o_pair, axis=0).T, jnp.concatenate(l_pair, axis=0).T)
                    o_pair, l_pair = [], []
            return carry

        lax.fori_loop(0, dil * n_tiles, tile, 0)

    for c in range(slabs):
        o_ref[0, :, c * LANES:(c + 1) * LANES] = acc_scr[c].astype(o_ref.dtype)


def _dil_attn(qkv9, seq):
    b = qkv9[0].shape[0]
    in_specs = [pl.BlockSpec((1,) + t.shape[1:], lambda bi: (bi, 0, 0, 0)) for t in qkv9]
    slabs = DIL_GROUP_WIDTH // LANES
    return pl.pallas_call(
        functools.partial(_dil_attn_kernel, seq=seq, tq=TQ_BAND),
        out_shape=jax.ShapeDtypeStruct((b, seq, DIL_GROUP_WIDTH), BF16),
        grid=(b,),
        in_specs=in_specs,
        out_specs=pl.BlockSpec((1, seq, DIL_GROUP_WIDTH), lambda bi: (bi, 0, 0)),
        scratch_shapes=[pltpu.VMEM((slabs, seq, LANES), F32), pltpu.VMEM((slabs, seq, LANES), F32),
                        pltpu.VMEM((DIL_HEADS_PER_GROUP * VT_ROWS, seq), BF16)],
        compiler_params=_params(1),
        name="dil_attn",
    )(*qkv9)


def _out_ffn_kernel(x_ref, o_ref, wo_ref, fn_ref, wg_ref, wu_ref, wd_ref, y_ref):
    subs = [slice(r0, r0 + SUB_FFN) for r0 in range(0, TM_FFN, SUB_FFN)]
    x1s = [x_ref[rows, :] + jnp.dot(o_ref[rows, :], wo_ref[...], preferred_element_type=F32) for rows in subs]
    hs = [_rms(x1, fn_ref[...]).astype(BF16) for x1 in x1s]
    gate_up = [(jnp.dot(h, wg_ref[...], preferred_element_type=F32),
                jnp.dot(h, wu_ref[...], preferred_element_type=F32)) for h in hs]
    acts = [(gate * jax.nn.sigmoid(gate) * up).astype(BF16) for gate, up in gate_up]
    for rows, x1, act in zip(subs, x1s, acts):
        y_ref[rows, :] = x1 + jnp.dot(act, wd_ref[...], preferred_element_type=F32)


def _out_ffn(x, o, wo, fn, wg, wu, wd):
    m = x.shape[0]
    tm = TM_FFN
    row = lambda i: (i, 0)
    in_specs = [pl.BlockSpec((tm, D_MODEL), row), pl.BlockSpec((tm, o.shape[1]), row),
                _resident(wo.shape), _resident(fn.shape), _resident(wg.shape), _resident(wu.shape),
                _resident(wd.shape)]
    return pl.pallas_call(
        _out_ffn_kernel,
        out_shape=jax.ShapeDtypeStruct((m, D_MODEL), F32),
        grid=(m // tm,),
        in_specs=in_specs,
        out_specs=pl.BlockSpec((tm, D_MODEL), row),
        compiler_params=_params(1),
        name="out_ffn",
    )(x, o, wo, fn, wg, wu, wd)


def _rope_tables(seq, dim, lanes):
    inv = 1.0 / (ROPE_THETA ** (jnp.arange(0, dim, 2, dtype=F32) / dim))
    ang = jnp.arange(seq, dtype=F32)[:, None] * inv[None, :]
    cos, sin = jnp.cos(ang), jnp.sin(ang)
    zero = jnp.zeros_like(sin)
    tabs = (jnp.concatenate([cos, cos], -1), jnp.concatenate([-sin, zero], -1),
            jnp.concatenate([zero, sin], -1))
    return tuple(jnp.tile(t, (1, lanes // dim)) for t in tabs)


def _row(v):
    return v.reshape(1, -1).astype(F32)


def _with_rope_partner(t):
    half = MLA_ROPE // 2
    zeros = jnp.zeros(t.shape[:-1] + (LANES - MLA_ROPE,), t.dtype)
    partner = jnp.concatenate([t[..., half:], t[..., :half]], -1)
    return jnp.concatenate([t, zeros, partner, zeros], -1)


def _head_gain(g, reps, scale=1.0):
    return jnp.tile(g.astype(F32) * scale, reps).reshape(1, -1)


def _trunk(x, attn_norm, ffn_norm, w_gate, w_up, w_down,
           mla_wq_a, mla_q_a_norm, mla_wq_b, mla_wkv_a, mla_kv_a_norm, mla_wkv_b,
           mla_q_norm, mla_k_norm, mla_wo,
           swa_wqkv, swa_q_norm, swa_k_norm, swa_sink, swa_wo,
           dil_wqkv, dil_q_norm, dil_k_norm, dil_wo):
    b, seq, _ = x.shape
    m = b * seq
    xf = x.reshape(m, D_MODEL)
    tabs64 = _rope_tables(seq, HEAD_DIM, LANES)
    cos, msin, psin = _rope_tables(seq, MLA_ROPE, MLA_ROPE)
    pad = ((0, 0), (0, LANES - MLA_ROPE))
    tabs_mla = (jnp.pad(cos, pad), jnp.pad(msin + psin, pad))
    blk = jnp.arange(NORM_CHUNK) // HEAD_DIM
    gmat = ((blk[:, None] == blk[None, :]).astype(F32) / HEAD_DIM).astype(BF16)
    reps = NORM_CHUNK // HEAD_DIM

    for i in range(DEPTH):
        kind, j = i % N_MIXERS, i // N_MIXERS
        an = _row(attn_norm[i])
        if kind == 0:
            wqb = mla_wq_b[j].reshape(MLA_Q_LORA, MLA_HEADS, MLA_QK)
            wqb = jnp.concatenate([wqb[..., :MLA_NOPE], _with_rope_partner(wqb[..., MLA_NOPE:])], -1)
            wkva = jnp.concatenate([mla_wkv_a[j][:, :MLA_KV_LORA],
                                    _with_rope_partner(mla_wkv_a[j][:, MLA_KV_LORA:])], -1)
            gains = [jnp.concatenate([g[:MLA_NOPE], _with_rope_partner(g[MLA_NOPE:])])
                     for g in (mla_q_norm[j], mla_k_norm[j])]
            q, k, v = _mla_proj(
                xf, seq, an, mla_wq_a[j].astype(BF16), _row(mla_q_a_norm[j]),
                wqb.reshape(MLA_Q_LORA, -1).astype(BF16), wkva.astype(BF16), _row(mla_kv_a_norm[j]),
                mla_wkv_b[j].astype(BF16), _row(gains[0]), _row(gains[1]), tabs_mla)
            o = _mla_attn(q.reshape(b, seq, -1), k.reshape(b, seq, -1), v.reshape(b, seq, -1))
            wo = mla_wo[j]
        elif kind == 1:
            nq, nk = SWA_Q_HEADS * SWA_HEAD_DIM, SWA_KV_HEADS * SWA_HEAD_DIM
            q, k, v = _swa_proj(xf, seq, an, swa_wqkv[j].astype(BF16),
                                _head_gain(swa_q_norm[j], reps, SWA_HEAD_DIM ** -0.5 * LOG2E),
                                _head_gain(swa_k_norm[j], reps), gmat, tabs64, nq, nk)
            o = _swa_attn(q.reshape(b, seq, nq), k.reshape(b, seq, nk), v.reshape(b, seq, nk),
                          _row(swa_sink[j]) * LOG2E)
            wo = swa_wo[j]
        else:
            qkv9 = _dil_proj(xf, b, seq, an, dil_wqkv[j].astype(BF16),
                             _head_gain(dil_q_norm[j], reps, DIL_HEAD_DIM ** -0.5 * LOG2E),
                             _head_gain(dil_k_norm[j], reps), gmat, tabs64)
            o = _dil_attn(qkv9, seq)
            wo = dil_wo[j]
        xf = _out_ffn(xf, o.reshape(m, -1), wo.astype(BF16), _row(ffn_norm[i]), w_gate[i].astype(BF16),
                      w_up[i].astype(BF16), w_down[i].astype(BF16))
    return xf.reshape(b, seq, D_MODEL)


def kernel(x_prompt, x_sample, attn_norm, ffn_norm, w_gate, w_up, w_down, mla_wq_a, mla_q_a_norm, mla_wq_b, mla_wkv_a, mla_kv_a_norm, mla_wkv_b, mla_q_norm, mla_k_norm, mla_wo, swa_wqkv, swa_q_norm, swa_k_norm, swa_sink, swa_wo, dil_wqkv, dil_q_norm, dil_k_norm, dil_wo):
    weights = (attn_norm, ffn_norm, w_gate, w_up, w_down,
               mla_wq_a, mla_q_a_norm, mla_wq_b, mla_wkv_a, mla_kv_a_norm, mla_wkv_b,
               mla_q_norm, mla_k_norm, mla_wo,
               swa_wqkv, swa_q_norm, swa_k_norm, swa_sink, swa_wo,
               dil_wqkv, dil_q_norm, dil_k_norm, dil_wo)
    return (_trunk(x_prompt, *weights), _trunk(x_sample, *weights))
```

```python
import functools
import math

import jax
import jax.numpy as jnp
from jax import lax
from jax.experimental import pallas as pl
from jax.experimental.pallas import tpu as pltpu

F32 = jnp.float32
BF16 = jnp.bfloat16

D_MODEL = 1024
DEPTH = 4
N_MIXERS = 3
ROPE_THETA = 10000.0
NORM_EPS = 1e-6
NEG_BIG = -1e30
LOG2E = math.log2(math.e)

MLA_HEADS = 8
MLA_Q_LORA = 384
MLA_KV_LORA = 256
MLA_NOPE = 128
MLA_ROPE = 64
MLA_V = 128
MLA_QK = MLA_NOPE + MLA_ROPE
MLA_QK_PAD = 256

SWA_Q_HEADS = 16
SWA_KV_HEADS = 4
SWA_HEAD_DIM = 64
SWA_HALF_WINDOW = 128

DIL_CONFIGS = ((128, 1), (512, 4), (2048, 16))
DIL_GROUPS = len(DIL_CONFIGS)
DIL_HEADS_PER_GROUP = 8
DIL_HEAD_DIM = 64
DIL_GROUP_WIDTH = DIL_HEADS_PER_GROUP * DIL_HEAD_DIM

D_FF = 2816

LANES = 128
HEAD_DIM = 64
NORM_CHUNK = 256
VMEM_LIMIT = 56 * 1024 * 1024

TM_PROJ = 1024
STAGE_SETS = 2
SUB_PROJ = 256
TM_FFN = 512
SUB_FFN = 256
TQ_MLA = 256
MLA_HEADS_PER_STEP = 2
TQ_BAND = 256
DIL_SUBSEQ_PER_TRIP = 4
PIPE_DEPTH = 3
PROJ_DEPTH = 2
VT_ROWS = 80
NT_DIMS = (((1,), (1,)), ((), ()))


def _rms(x, g):
    return x * lax.rsqrt(jnp.mean(x * x, axis=-1, keepdims=True) + NORM_EPS) * g


def _rope(v, c, s1, s2):
    return v * c + pltpu.roll(v, LANES - 32, 1) * s1 + pltpu.roll(v, 32, 1) * s2


def _resident(shape):
    return pl.BlockSpec(shape, lambda *_: (0,) * len(shape), pipeline_mode=pl.Buffered(1))


def _params(n_axes):
    return pltpu.CompilerParams(dimension_semantics=("parallel",) * n_axes,
                                vmem_limit_bytes=VMEM_LIMIT)


def _mla_proj_kernel(x_ref, an_ref, wqa_ref, qan_ref, wqb_ref, wkva_ref, kvan_ref, wkvb_ref,
                     gq_ref, gk_ref, c_ref, s_ref, ones_ref, q_out, k_out, v_out):
    gq, gk = gq_ref[...], gk_ref[...]
    scale = MLA_QK ** -0.5 * LOG2E
    head_w = MLA_QK_PAD + LANES

    def rms(t, g):
        n = t.shape[1]
        ms = jnp.dot((t * t).astype(BF16), ones_ref[:n, :], preferred_element_type=F32) * (1.0 / n)
        return t * jnp.concatenate([lax.rsqrt(ms + NORM_EPS)] * (n // LANES), axis=1) * g

    def project(rows):
        h = rms(x_ref[rows, :], an_ref[...]).astype(BF16)
        cq = jnp.dot(h, wqa_ref[...], preferred_element_type=F32)
        cq = rms(cq, qan_ref[...]).astype(BF16)
        qf = jnp.dot(cq, wqb_ref[...], preferred_element_type=F32)
        kva = jnp.dot(h, wkva_ref[...], preferred_element_type=F32)
        ckv = rms(kva[:, :MLA_KV_LORA], kvan_ref[...]).astype(BF16)
        kv = jnp.dot(ckv, wkvb_ref[...], preferred_element_type=F32)
        return qf, kva, kv

    def heads(rows, qf, kva, kv):
        c, s = c_ref[rows, :], s_ref[rows, :]
        kr = kva[:, MLA_KV_LORA:MLA_KV_LORA + LANES]
        kr_ss = jnp.sum(kr * kr, axis=-1, keepdims=True)
        kr_roped = kr * gk[:, LANES:2 * LANES] * c + kva[:, MLA_KV_LORA + LANES:] * gk[:, 2 * LANES:] * s
        for hh in range(MLA_HEADS):
            lo, out = hh * head_w, hh * MLA_QK_PAD
            qa, qb, qp = (qf[:, lo + i * LANES:lo + (i + 1) * LANES] for i in range(3))
            rq = lax.rsqrt(jnp.sum(qa * qa + qb * qb, axis=-1, keepdims=True) * (1.0 / MLA_QK) + NORM_EPS) * scale
            q_out[rows, out:out + LANES] = (qa * rq * gq[:, :LANES]).astype(BF16)
            q_out[rows, out + LANES:out + MLA_QK_PAD] = (
                (qb * gq[:, LANES:2 * LANES] * c + qp * gq[:, 2 * LANES:] * s) * rq).astype(BF16)
            kn = kv[:, out:out + LANES]
            rk = lax.rsqrt((jnp.sum(kn * kn, axis=-1, keepdims=True) + kr_ss) * (1.0 / MLA_QK) + NORM_EPS)
            k_out[rows, out:out + LANES] = (kn * rk * gk[:, :LANES]).astype(BF16)
            k_out[rows, out + LANES:out + MLA_QK_PAD] = (kr_roped * rk).astype(BF16)
            v_out[rows, hh * MLA_V:(hh + 1) * MLA_V] = kv[:, out + LANES:out + MLA_QK_PAD].astype(BF16)

    subs = [slice(r0, r0 + SUB_PROJ) for r0 in range(0, TM_PROJ, SUB_PROJ)]
    pending = project(subs[0])
    for si, rows in enumerate(subs):
        current = pending
        if si + 1 < len(subs):
            pending = project(subs[si + 1])
        heads(rows, *current)


def _mla_proj(x, seq, an, wqa, qan, wqb, wkva, kvan, wkvb, gq, gk, tabs):
    m = x.shape[0]
    tm = TM_PROJ
    nblk = seq // tm
    row = lambda i: (i, 0)
    pos = lambda i: (i % nblk, 0)
    hq = MLA_HEADS * MLA_QK_PAD
    in_specs = [pl.BlockSpec((tm, D_MODEL), row), _resident(an.shape), _resident(wqa.shape),
                _resident(qan.shape), _resident(wqb.shape), _resident(wkva.shape),
                _resident(kvan.shape), _resident(wkvb.shape), _resident(gq.shape), _resident(gk.shape)]
    in_specs += [pl.BlockSpec((tm, LANES), pos)] * 2
    ones = jnp.ones((D_MODEL, LANES), BF16)
    in_specs.append(_resident(ones.shape))
    return pl.pallas_call(
        _mla_proj_kernel,
        out_shape=(jax.ShapeDtypeStruct((m, hq), BF16), jax.ShapeDtypeStruct((m, hq), BF16),
                   jax.ShapeDtypeStruct((m, MLA_HEADS * MLA_V), BF16)),
        grid=(m // tm,),
        in_specs=in_specs,
        out_specs=(pl.BlockSpec((tm, hq), row), pl.BlockSpec((tm, hq), row),
                   pl.BlockSpec((tm, MLA_HEADS * MLA_V), row)),
        compiler_params=_params(1),
        name="mla_proj",
    )(x, an, wqa, qan, wqb, wkva, kvan, wkvb, gq, gk, *tabs, ones)


def _mla_attn_kernel(q_ref, k_ref, v_ref, o_ref, v1_scr, *, seq, tq):
    hps = MLA_HEADS_PER_STEP
    for h in range(hps):
        v1_scr[h, :, :MLA_V] = v_ref[0, :, h * MLA_V:(h + 1) * MLA_V]
        v1_scr[h, :, MLA_V:] = jnp.ones((seq, MLA_V), BF16)

    def scores(h, t):
        cols = slice(h * MLA_QK_PAD, (h + 1) * MLA_QK_PAD)
        return lax.dot_general(q_ref[0, t * tq:(t + 1) * tq, cols], k_ref[0, :, cols], NT_DIMS,
                               preferred_element_type=F32)

    def finish(s, h, t):
        m = jnp.max(s, axis=-1, keepdims=True)
        p = jnp.exp2(s - m).astype(BF16)
        o = jnp.dot(p, v1_scr[h], preferred_element_type=F32)
        o_ref[0, t * tq:(t + 1) * tq, h * MLA_V:(h + 1) * MLA_V] = (
            o[:, :MLA_V] / o[:, MLA_V:]).astype(o_ref.dtype)

    tiles = [(h, t) for h in range(hps) for t in range(seq // tq)]
    queue = [scores(*ht) for ht in tiles[:PIPE_DEPTH]]
    for i, ht in enumerate(tiles):
        s = queue.pop(0)
        if i + PIPE_DEPTH < len(tiles):
            queue.append(scores(*tiles[i + PIPE_DEPTH]))
        finish(s, *ht)


def _mla_attn(q, k, v):
    b, seq, _ = q.shape
    hps = MLA_HEADS_PER_STEP
    head = lambda bi, hi: (bi, 0, hi)
    return pl.pallas_call(
        functools.partial(_mla_attn_kernel, seq=seq, tq=TQ_MLA),
        out_shape=jax.ShapeDtypeStruct((b, seq, MLA_HEADS * MLA_V), BF16),
        grid=(b, MLA_HEADS // hps),
        in_specs=[pl.BlockSpec((1, seq, hps * MLA_QK_PAD), head),
                  pl.BlockSpec((1, seq, hps * MLA_QK_PAD), head),
                  pl.BlockSpec((1, seq, hps * MLA_V), head)],
        out_specs=pl.BlockSpec((1, seq, hps * MLA_V), head),
        scratch_shapes=[pltpu.VMEM((hps, seq, 2 * MLA_V), BF16)],
        compiler_params=_params(2),
        name="mla_attn",
    )(q, k, v)


def _projected_chunks(x_ref, an_ref, w_ref):
    offsets = list(range(0, w_ref.shape[1], NORM_CHUNK))
    for r0 in range(0, TM_PROJ, SUB_PROJ):
        rows = slice(r0, r0 + SUB_PROJ)
        h = _rms(x_ref[rows, :], an_ref[...]).astype(BF16)
        chunk = lambda c0: jnp.dot(h, w_ref[:, c0:c0 + NORM_CHUNK], preferred_element_type=F32)
        queue = [chunk(c0) for c0 in offsets[:PROJ_DEPTH]]
        for i, c0 in enumerate(offsets):
            t = queue.pop(0)
            if i + PROJ_DEPTH < len(offsets):
                queue.append(chunk(offsets[i + PROJ_DEPTH]))
            yield rows, c0, t


def _normed_roped_halves(t, g, gmat, c, s1, s2):
    ms = jnp.dot((t * t).astype(BF16), gmat, preferred_element_type=F32)
    tn = t * lax.rsqrt(ms + NORM_EPS) * g
    for half in range(NORM_CHUNK // LANES):
        yield half * LANES, _rope(tn[:, half * LANES:(half + 1) * LANES], c, s1, s2)


def _swa_proj_kernel(x_ref, an_ref, w_ref, gq_ref, gk_ref, gmat_ref, c_ref, s1_ref, s2_ref,
                     q_out, k_out, v_out, *, nq, nk):
    gmat = gmat_ref[...]
    for rows, c0, t in _projected_chunks(x_ref, an_ref, w_ref):
        if c0 >= nq + nk:
            v_out[rows, c0 - nq - nk:c0 - nq - nk + NORM_CHUNK] = t.astype(BF16)
            continue
        out, g, lo = (q_out, gq_ref, c0) if c0 < nq else (k_out, gk_ref, c0 - nq)
        tabs = (c_ref[rows, :], s1_ref[rows, :], s2_ref[rows, :])
        for off, val in _normed_roped_halves(t, g[...], gmat, *tabs):
            out[rows, lo + off:lo + off + LANES] = val.astype(BF16)


def _swa_proj(x, seq, an, w, gq, gk, gmat, tabs, nq, nk):
    m = x.shape[0]
    tm = TM_PROJ
    nblk = seq // tm
    nv = w.shape[1] - nq - nk
    row = lambda i: (i, 0)
    pos = lambda i: (i % nblk, 0)
    in_specs = [pl.BlockSpec((tm, D_MODEL), row), _resident(an.shape), _resident(w.shape),
                _resident(gq.shape), _resident(gk.shape), _resident(gmat.shape)]
    in_specs += [pl.BlockSpec((tm, LANES), pos)] * 3
    return pl.pallas_call(
        functools.partial(_swa_proj_kernel, nq=nq, nk=nk),
        out_shape=(jax.ShapeDtypeStruct((m, nq), BF16), jax.ShapeDtypeStruct((m, nk), BF16),
                   jax.ShapeDtypeStruct((m, nv), BF16)),
        grid=(m // tm,),
        in_specs=in_specs,
        out_specs=(pl.BlockSpec((tm, nq), row), pl.BlockSpec((tm, nk), row), pl.BlockSpec((tm, nv), row)),
        compiler_params=_params(1),
        name="swa_proj",
    )(x, an, w, gq, gk, gmat, *tabs)


def _dil_proj_kernel(x_ref, an_ref, w_ref, gq_ref, gk_ref, gmat_ref, c_ref, s1_ref, s2_ref, *refs):
    outs, stage = refs[:3 * DIL_GROUPS], refs[3 * DIL_GROUPS]
    gmat = gmat_ref[...]
    nq = DIL_GROUPS * DIL_GROUP_WIDTH
    slabs_per_group = DIL_GROUP_WIDTH // LANES
    for rows, c0, t in _projected_chunks(x_ref, an_ref, w_ref):
        r0 = rows.start
        sub = r0 // SUB_PROJ
        tensor, lo0 = c0 // nq, c0 % nq

        def emit(tensor, lo, val, sub=sub, r0=r0):
            gi, c = lo // DIL_GROUP_WIDTH, (lo % DIL_GROUP_WIDTH) // LANES
            dil = DIL_CONFIGS[gi][1]
            out = outs[3 * gi + tensor]
            cols = slice(c * LANES, (c + 1) * LANES)
            if dil == 1:
                out[0, 0, r0:r0 + SUB_PROJ, cols] = val.astype(BF16)
                return
            slot = ((sub % STAGE_SETS * 3 + tensor) * (DIL_GROUPS - 1) + gi - 1) * slabs_per_group + c
            stage[slot] = val
            n = SUB_PROJ // dil
            for r in range(dil):
                out[0, r, r0 // dil:r0 // dil + n, cols] = stage[slot, pl.ds(r, n, stride=dil), :].astype(BF16)

        if tensor == 2:
            for off in range(0, NORM_CHUNK, LANES):
                emit(2, lo0 + off, t[:, off:off + LANES])
            continue
        tabs = (c_ref[rows, :], s1_ref[rows, :], s2_ref[rows, :])
        for off, val in _normed_roped_halves(t, (gq_ref, gk_ref)[tensor][...], gmat, *tabs):
            emit(tensor, lo0 + off, val)


def _dil_proj(x, b, seq, an, w, gq, gk, gmat, tabs):
    m = x.shape[0]
    tm = TM_PROJ
    nblk = seq // tm
    row = lambda i: (i, 0)
    pos = lambda i: (i % nblk, 0)
    in_specs = [pl.BlockSpec((tm, D_MODEL), row), _resident(an.shape), _resident(w.shape),
                _resident(gq.shape), _resident(gk.shape), _resident(gmat.shape)]
    in_specs += [pl.BlockSpec((tm, LANES), pos)] * 3
    out_shape, out_specs = [], []
    for _, dil in DIL_CONFIGS:
        for _ in range(3):
            out_shape.append(jax.ShapeDtypeStruct((b, dil, seq // dil, DIL_GROUP_WIDTH), BF16))
            out_specs.append(pl.BlockSpec((1, dil, tm // dil, DIL_GROUP_WIDTH),
                                          lambda i: (i // nblk, 0, i % nblk, 0)))
    n_slots = STAGE_SETS * 3 * (DIL_GROUPS - 1) * (DIL_GROUP_WIDTH // LANES)
    return pl.pallas_call(
        _dil_proj_kernel,
        out_shape=tuple(out_shape),
        grid=(m // tm,),
        in_specs=in_specs,
        out_specs=tuple(out_specs),
        scratch_shapes=[pltpu.VMEM((n_slots, SUB_PROJ, LANES), F32)],
        compiler_params=_params(1),
        name="dil_proj",
    )(x, an, w, gq, gk, gmat, *tabs)


def _band_bias_t(win, tq, off, half):
    dd = (lax.broadcasted_iota(jnp.int32, (win, tq), 0)
          - lax.broadcasted_iota(jnp.int32, (win, tq), 1))
    return jnp.where(jnp.abs(dd + off) <= half, 0.0, NEG_BIG).astype(F32)


def _fill_vt(vt_scr, v_block, head0, cols):
    d = HEAD_DIM
    vt = v_block.astype(F32).T.astype(BF16)
    for e in range(LANES // d):
        lo = (head0 + e) * VT_ROWS
        vt_scr[lo:lo + d, cols] = vt[e * d:(e + 1) * d, :]
        vt_scr[lo + d:lo + VT_ROWS, cols] = jnp.ones((VT_ROWS - d, vt.shape[1]), BF16)


def _probs_t(st, sk2):
    m = jnp.max(st, axis=0, keepdims=True)
    if sk2 is not None:
        m = jnp.maximum(m, sk2)
    return jnp.exp2(st - m).astype(BF16), m


def _swa_attn_kernel(q_ref, k_ref, v_ref, sink_ref, o_ref, vt_scr, *, length, tq, win, half, q_heads, group):
    d = HEAD_DIM
    chunk = 4 * LANES
    for j in range(q_heads // group * d // LANES):
        for c0 in range(0, length, chunk):
            _fill_vt(vt_scr, v_ref[0, c0:c0 + chunk, j * LANES:(j + 1) * LANES], 2 * j, slice(c0, c0 + chunk))

    def body(i, carry):
        q0 = pl.multiple_of(i * tq, tq)
        start = pl.multiple_of(jnp.clip(q0 - half, 0, length - win), LANES)
        bias_t = _band_bias_t(win, tq, start - q0, half)

        def scores(a):
            kg = k_ref[0, pl.ds(start, win), (a // group) * d:(a // group + 1) * d]
            qa = q_ref[0, pl.ds(q0, tq), a * d:(a + 1) * d]
            return lax.dot_general(kg, qa, NT_DIMS, preferred_element_type=F32) + bias_t

        queue = [scores(a) for a in range(PIPE_DEPTH)]
        pair = []
        for a in range(q_heads):
            st = queue.pop(0)
            if a + PIPE_DEPTH < q_heads:
                queue.append(scores(a + PIPE_DEPTH))
            sk2 = sink_ref[:, a:a + 1]
            pt, m = _probs_t(st, sk2)
            g = a // group
            o2t = jnp.dot(vt_scr[g * VT_ROWS:(g + 1) * VT_ROWS, pl.ds(start, win)], pt,
                          preferred_element_type=F32)
            pair.append(o2t[:d, :] / (o2t[d:d + 1, :] + jnp.exp2(sk2 - m)))
            if len(pair) == LANES // d:
                o_ref[0, pl.ds(q0, tq), (a - 1) * d:(a + 1) * d] = (
                    jnp.concatenate(pair, axis=0).T.astype(o_ref.dtype))
                pair = []
        return carry

    lax.fori_loop(0, length // tq, body, 0)


def _swa_attn(q, k, v, sink):
    b, length, nq = q.shape
    nk = k.shape[-1]
    tq, half = TQ_BAND, SWA_HALF_WINDOW
    whole = lambda bi: (bi, 0, 0)
    return pl.pallas_call(
        functools.partial(_swa_attn_kernel, length=length, tq=tq, win=tq + 2 * half, half=half,
                          q_heads=SWA_Q_HEADS, group=SWA_Q_HEADS // SWA_KV_HEADS),
        out_shape=jax.ShapeDtypeStruct((b, length, nq), BF16),
        grid=(b,),
        in_specs=[pl.BlockSpec((1, length, nq), whole), pl.BlockSpec((1, length, nk), whole),
                  pl.BlockSpec((1, length, nk), whole), _resident(sink.shape)],
        out_specs=pl.BlockSpec((1, length, nq), whole),
        scratch_shapes=[pltpu.VMEM((SWA_KV_HEADS * VT_ROWS, length), BF16)],
        compiler_params=_params(1),
        name="swa_attn",
    )(q, k, v, sink)


def _dil_attn_kernel(*refs, seq, tq):
    qkv, o_ref = refs[:3 * DIL_GROUPS], refs[3 * DIL_GROUPS]
    acc_scr, lse_scr, vt_scr = refs[3 * DIL_GROUPS + 1:]
    d, heads = DIL_HEAD_DIM, DIL_HEADS_PER_GROUP
    slabs = DIL_GROUP_WIDTH // LANES

    for gi, (window, dil) in enumerate(DIL_CONFIGS):
        q_ref, k_ref, v_ref = qkv[3 * gi:3 * gi + 3]
        length = seq // dil
        half = window // (2 * dil)
        t_q = min(tq, length)
        win = min(length, t_q + 2 * LANES)
        n_tiles = length // t_q

        def fill(r, carry, v_ref=v_ref, length=length):
            cols = pl.ds(pl.multiple_of(r * length, LANES), length)
            for j in range(slabs):
                _fill_vt(vt_scr, v_ref[0, r, :, j * LANES:(j + 1) * LANES], 2 * j, cols)
            return carry

        lax.fori_loop(0, dil, fill, 0)

        per_trip = DIL_SUBSEQ_PER_TRIP if n_tiles == 1 else 1

        def tile(idx, carry, gi=gi, dil=dil, length=length, half=half, t_q=t_q, win=win,
                 n_tiles=n_tiles, per_trip=per_trip, q_ref=q_ref, k_ref=k_ref):
            q0 = pl.multiple_of((idx % n_tiles) * t_q, t_q)
            start = pl.multiple_of(jnp.clip(q0 - LANES, 0, length - win), LANES)
            bias_t = _band_bias_t(win, t_q, start - q0, half)
            residue = lambda j: (idx // n_tiles) * per_trip + j

            def scores(item):
                j, hh = item
                qa = q_ref[0, residue(j), pl.ds(q0, t_q), hh * d:(hh + 1) * d]
                kg = k_ref[0, residue(j), pl.ds(start, win), hh * d:(hh + 1) * d]
                return lax.dot_general(kg, qa, NT_DIMS, preferred_element_type=F32) + bias_t

            def merge(j, c, o_new, l_new):
                rows = pl.ds(q0 * dil + residue(j), t_q, stride=dil) if dil > 1 else pl.ds(q0, t_q)
                if gi == 0:
                    acc_scr[c, rows, :] = o_new
                    lse_scr[c, rows, :] = l_new
                    return
                l_old = lse_scr[c, rows, :]
                mx = jnp.maximum(l_old, l_new)
                wa = jnp.exp2(l_old - mx)
                wb = jnp.exp2(l_new - mx)
                acc_scr[c, rows, :] = (wa * acc_scr[c, rows, :] + wb * o_new) / (wa + wb)
                if gi + 1 < DIL_GROUPS:
                    lse_scr[c, rows, :] = mx + jnp.log2(wa + wb)

            items = [(j, hh) for j in range(per_trip) for hh in range(heads)]
            queue = [scores(item) for item in items[:PIPE_DEPTH]]
            o_pair, l_pair = [], []
            for i, (j, hh) in enumerate(items):
                st = queue.pop(0)
                if i + PIPE_DEPTH < len(items):
                    queue.append(scores(items[i + PIPE_DEPTH]))
                pt, m = _probs_t(st, None)
                vcols = pl.ds(pl.multiple_of(residue(j) * length + start, LANES), win)
                o2t = jnp.dot(vt_scr[hh * VT_ROWS:(hh + 1) * VT_ROWS, vcols], pt,
                              preferred_element_type=F32)
                den = o2t[d:d + 1, :]
                o_pair.append(o2t[:d, :] / den)
                l_pair.append(jnp.broadcast_to(m + jnp.log2(den), (d, t_q)))
                if len(o_pair) == LANES // d:
                    merge(j, hh // 2, jnp.concatenate(o_pair, axis=0).T, jnp.concatenate(l_pair, axis=0).T)
                    o_pair, l_pair = [], []
            return carry

        lax.fori_loop(0, dil * n_tiles // per_trip, tile, 0)

    for c in range(slabs):
        o_ref[0, :, c * LANES:(c + 1) * LANES] = acc_scr[c].astype(o_ref.dtype)


def _dil_attn(qkv9, seq):
    b = qkv9[0].shape[0]
    in_specs = [pl.BlockSpec((1,) + t.shape[1:], lambda bi: (bi, 0, 0, 0)) for t in qkv9]
    slabs = DIL_GROUP_WIDTH // LANES
    return pl.pallas_call(
        functools.partial(_dil_attn_kernel, seq=seq, tq=TQ_BAND),
        out_shape=jax.ShapeDtypeStruct((b, seq, DIL_GROUP_WIDTH), BF16),
        grid=(b,),
        in_specs=in_specs,
        out_specs=pl.BlockSpec((1, seq, DIL_GROUP_WIDTH), lambda bi: (bi, 0, 0)),
        scratch_shapes=[pltpu.VMEM((slabs, seq, LANES), F32), pltpu.VMEM((slabs, seq, LANES), F32),
                        pltpu.VMEM((DIL_HEADS_PER_GROUP * VT_ROWS, seq), BF16)],
        compiler_params=_params(1),
        name="dil_attn",
    )(*qkv9)


def _out_ffn_kernel(x_ref, o_ref, wo_ref, fn_ref, wg_ref, wu_ref, wd_ref, y_ref):
    subs = [slice(r0, r0 + SUB_FFN) for r0 in range(0, TM_FFN, SUB_FFN)]
    x1s = [x_ref[rows, :] + jnp.dot(o_ref[rows, :], wo_ref[...], preferred_element_type=F32) for rows in subs]
    hs = [_rms(x1, fn_ref[...]).astype(BF16) for x1 in x1s]
    gate_up = [(jnp.dot(h, wg_ref[...], preferred_element_type=F32),
                jnp.dot(h, wu_ref[...], preferred_element_type=F32)) for h in hs]
    acts = [(gate * jax.nn.sigmoid(gate) * up).astype(BF16) for gate, up in gate_up]
    for rows, x1, act in zip(subs, x1s, acts):
        y_ref[rows, :] = x1 + jnp.dot(act, wd_ref[...], preferred_element_type=F32)


def _out_ffn(x, o, wo, fn, wg, wu, wd):
    m = x.shape[0]
    tm = TM_FFN
    row = lambda i: (i, 0)
    in_specs = [pl.BlockSpec((tm, D_MODEL), row), pl.BlockSpec((tm, o.shape[1]), row),
                _resident(wo.shape), _resident(fn.shape), _resident(wg.shape), _resident(wu.shape),
                _resident(wd.shape)]
    return pl.pallas_call(
        _out_ffn_kernel,
        out_shape=jax.ShapeDtypeStruct((m, D_MODEL), F32),
        grid=(m // tm,),
        in_specs=in_specs,
        out_specs=pl.BlockSpec((tm, D_MODEL), row),
        compiler_params=_params(1),
        name="out_ffn",
    )(x, o, wo, fn, wg, wu, wd)


def _rope_tables(seq, dim, lanes):
    inv = 1.0 / (ROPE_THETA ** (jnp.arange(0, dim, 2, dtype=F32) / dim))
    ang = jnp.arange(seq, dtype=F32)[:, None] * inv[None, :]
    cos, sin = jnp.cos(ang), jnp.sin(ang)
    zero = jnp.zeros_like(sin)
    tabs = (jnp.concatenate([cos, cos], -1), jnp.concatenate([-sin, zero], -1),
            jnp.concatenate([zero, sin], -1))
    return tuple(jnp.tile(t, (1, lanes // dim)) for t in tabs)


def _row(v):
    return v.reshape(1, -1).astype(F32)


def _with_rope_partner(t):
    half = MLA_ROPE // 2
    zeros = jnp.zeros(t.shape[:-1] + (LANES - MLA_ROPE,), t.dtype)
    partner = jnp.concatenate([t[..., half:], t[..., :half]], -1)
    return jnp.concatenate([t, zeros, partner, zeros], -1)


def _head_gain(g, reps, scale=1.0):
    return jnp.tile(g.astype(F32) * scale, reps).reshape(1, -1)


def _trunk(x, attn_norm, ffn_norm, w_gate, w_up, w_down,
           mla_wq_a, mla_q_a_norm, mla_wq_b, mla_wkv_a, mla_kv_a_norm, mla_wkv_b,
           mla_q_norm, mla_k_norm, mla_wo,
           swa_wqkv, swa_q_norm, swa_k_norm, swa_sink, swa_wo,
           dil_wqkv, dil_q_norm, dil_k_norm, dil_wo):
    b, seq, _ = x.shape
    m = b * seq
    xf = x.reshape(m, D_MODEL)
    tabs64 = _rope_tables(seq, HEAD_DIM, LANES)
    cos, msin, psin = _rope_tables(seq, MLA_ROPE, MLA_ROPE)
    pad = ((0, 0), (0, LANES - MLA_ROPE))
    tabs_mla = (jnp.pad(cos, pad), jnp.pad(msin + psin, pad))
    blk = jnp.arange(NORM_CHUNK) // HEAD_DIM
    gmat = ((blk[:, None] == blk[None, :]).astype(F32) / HEAD_DIM).astype(BF16)
    reps = NORM_CHUNK // HEAD_DIM

    for i in range(DEPTH):
        kind, j = i % N_MIXERS, i // N_MIXERS
        an = _row(attn_norm[i])
        if kind == 0:
            wqb = mla_wq_b[j].reshape(MLA_Q_LORA, MLA_HEADS, MLA_QK)
            wqb = jnp.concatenate([wqb[..., :MLA_NOPE], _with_rope_partner(wqb[..., MLA_NOPE:])], -1)
            wkva = jnp.concatenate([mla_wkv_a[j][:, :MLA_KV_LORA],
                                    _with_rope_partner(mla_wkv_a[j][:, MLA_KV_LORA:])], -1)
            gains = [jnp.concatenate([g[:MLA_NOPE], _with_rope_partner(g[MLA_NOPE:])])
                     for g in (mla_q_norm[j], mla_k_norm[j])]
            q, k, v = _mla_proj(
                xf, seq, an, mla_wq_a[j].astype(BF16), _row(mla_q_a_norm[j]),
                wqb.reshape(MLA_Q_LORA, -1).astype(BF16), wkva.astype(BF16), _row(mla_kv_a_norm[j]),
                mla_wkv_b[j].astype(BF16), _row(gains[0]), _row(gains[1]), tabs_mla)
            o = _mla_attn(q.reshape(b, seq, -1), k.reshape(b, seq, -1), v.reshape(b, seq, -1))
            wo = mla_wo[j]
        elif kind == 1:
            nq, nk = SWA_Q_HEADS * SWA_HEAD_DIM, SWA_KV_HEADS * SWA_HEAD_DIM
            q, k, v = _swa_proj(xf, seq, an, swa_wqkv[j].astype(BF16),
                                _head_gain(swa_q_norm[j], reps, SWA_HEAD_DIM ** -0.5 * LOG2E),
                                _head_gain(swa_k_norm[j], reps), gmat, tabs64, nq, nk)
            o = _swa_attn(q.reshape(b, seq, nq), k.reshape(b, seq, nk), v.reshape(b, seq, nk),
                          _row(swa_sink[j]) * LOG2E)
            wo = swa_wo[j]
        else:
            qkv9 = _dil_proj(xf, b, seq, an, dil_wqkv[j].astype(BF16),
                             _head_gain(dil_q_norm[j], reps, DIL_HEAD_DIM ** -0.5 * LOG2E),
                             _head_gain(dil_k_norm[j], reps), gmat, tabs64)
            o = _dil_attn(qkv9, seq)
            wo = dil_wo[j]
        xf = _out_ffn(xf, o.reshape(m, -1), wo.astype(BF16), _row(ffn_norm[i]), w_gate[i].astype(BF16),
                      w_up[i].astype(BF16), w_down[i].astype(BF16))
    return xf.reshape(b, seq, D_MODEL)


def kernel(x_prompt, x_sample, attn_norm, ffn_norm, w_gate, w_up, w_down, mla_wq_a, mla_q_a_norm, mla_wq_b, mla_wkv_a, mla_kv_a_norm, mla_wkv_b, mla_q_norm, mla_k_norm, mla_wo, swa_wqkv, swa_q_norm, swa_k_norm, swa_sink, swa_wo, dil_wqkv, dil_q_norm, dil_k_norm, dil_wo):
    weights = (attn_norm, ffn_norm, w_gate, w_up, w_down,
               mla_wq_a, mla_q_a_norm, mla_wq_b, mla_wkv_a, mla_kv_a_norm, mla_wkv_b,
               mla_q_norm, mla_k_norm, mla_wo,
               swa_wqkv, swa_q_norm, swa_k_norm, swa_sink, swa_wo,
               dil_wqkv, dil_q_norm, dil_k_norm, dil_wo)
    return (_trunk(x_prompt, *weights), _trunk(x_sample, *weights))
```

```python
import functools
import math

import jax
import jax.numpy as jnp
from jax import lax
from jax.experimental import pallas as pl
from jax.experimental.pallas import tpu as pltpu

F32 = jnp.float32
BF16 = jnp.bfloat16

D_MODEL = 1024
DEPTH = 4
N_MIXERS = 3
ROPE_THETA = 10000.0
NORM_EPS = 1e-6
NEG_BIG = -1e30
LOG2E = math.log2(math.e)

MLA_HEADS = 8
MLA_Q_LORA = 384
MLA_KV_LORA = 256
MLA_NOPE = 128
MLA_ROPE = 64
MLA_V = 128
MLA_QK = MLA_NOPE + MLA_ROPE
MLA_QK_PAD = 256

SWA_Q_HEADS = 16
SWA_KV_HEADS = 4
SWA_HEAD_DIM = 64
SWA_HALF_WINDOW = 128

DIL_CONFIGS = ((128, 1), (512, 4), (2048, 16))
DIL_GROUPS = len(DIL_CONFIGS)
DIL_HEADS_PER_GROUP = 8
DIL_HEAD_DIM = 64
DIL_GROUP_WIDTH = DIL_HEADS_PER_GROUP * DIL_HEAD_DIM

D_FF = 2816

LANES = 128
HEAD_DIM = 64
NORM_CHUNK = 256
VMEM_LIMIT = 56 * 1024 * 1024

TM_PROJ = 1024
STAGE_SETS = 2
SUB_PROJ = 256
TM_FFN = 512
SUB_FFN = 256
TQ_MLA = 256
MLA_HEADS_PER_STEP = 2
TQ_BAND = 256
DIL_SUBSEQ_PER_TRIP = 4
DIL_TILES_PER_TRIP = 2
SWA_TILES_PER_TRIP = 2
PIPE_DEPTH = 3
PROJ_DEPTH = 2
VT_ROWS = 80
NT_DIMS = (((1,), (1,)), ((), ()))


def _rms(x, g):
    return x * lax.rsqrt(jnp.mean(x * x, axis=-1, keepdims=True) + NORM_EPS) * g


def _rope(v, c, s1, s2):
    return v * c + pltpu.roll(v, LANES - 32, 1) * s1 + pltpu.roll(v, 32, 1) * s2


def _resident(shape):
    return pl.BlockSpec(shape, lambda *_: (0,) * len(shape), pipeline_mode=pl.Buffered(1))


def _params(n_axes):
    return pltpu.CompilerParams(dimension_semantics=("parallel",) * n_axes,
                                vmem_limit_bytes=VMEM_LIMIT)


def _mla_proj_kernel(x_ref, an_ref, wqa_ref, qan_ref, wqb_ref, wkva_ref, kvan_ref, wkvb_ref,
                     gq_ref, gk_ref, c_ref, s_ref, ones_ref, q_out, k_out, v_out):
    gq, gk = gq_ref[...], gk_ref[...]
    scale = MLA_QK ** -0.5 * LOG2E
    head_w = MLA_QK_PAD + LANES

    def rms(t, g):
        n = t.shape[1]
        ms = jnp.dot((t * t).astype(BF16), ones_ref[:n, :], preferred_element_type=F32) * (1.0 / n)
        return t * jnp.concatenate([lax.rsqrt(ms + NORM_EPS)] * (n // LANES), axis=1) * g

    def project(rows):
        h = rms(x_ref[rows, :], an_ref[...]).astype(BF16)
        cq = jnp.dot(h, wqa_ref[...], preferred_element_type=F32)
        cq = rms(cq, qan_ref[...]).astype(BF16)
        qf = jnp.dot(cq, wqb_ref[...], preferred_element_type=F32)
        kva = jnp.dot(h, wkva_ref[...], preferred_element_type=F32)
        ckv = rms(kva[:, :MLA_KV_LORA], kvan_ref[...]).astype(BF16)
        kv = jnp.dot(ckv, wkvb_ref[...], preferred_element_type=F32)
        return qf, kva, kv

    def heads(rows, qf, kva, kv):
        c, s = c_ref[rows, :], s_ref[rows, :]
        kr = kva[:, MLA_KV_LORA:MLA_KV_LORA + LANES]
        kr_ss = jnp.sum(kr * kr, axis=-1, keepdims=True)
        kr_roped = kr * gk[:, LANES:2 * LANES] * c + kva[:, MLA_KV_LORA + LANES:] * gk[:, 2 * LANES:] * s
        for hh in range(MLA_HEADS):
            lo, out = hh * head_w, hh * MLA_QK_PAD
            qa, qb, qp = (qf[:, lo + i * LANES:lo + (i + 1) * LANES] for i in range(3))
            rq = lax.rsqrt(jnp.sum(qa * qa + qb * qb, axis=-1, keepdims=True) * (1.0 / MLA_QK) + NORM_EPS) * scale
            q_out[rows, out:out + LANES] = (qa * rq * gq[:, :LANES]).astype(BF16)
            q_out[rows, out + LANES:out + MLA_QK_PAD] = (
                (qb * gq[:, LANES:2 * LANES] * c + qp * gq[:, 2 * LANES:] * s) * rq).astype(BF16)
            kn = kv[:, out:out + LANES]
            rk = lax.rsqrt((jnp.sum(kn * kn, axis=-1, keepdims=True) + kr_ss) * (1.0 / MLA_QK) + NORM_EPS)
            k_out[rows, out:out + LANES] = (kn * rk * gk[:, :LANES]).astype(BF16)
            k_out[rows, out + LANES:out + MLA_QK_PAD] = (kr_roped * rk).astype(BF16)
            v_out[rows, hh * MLA_V:(hh + 1) * MLA_V] = kv[:, out + LANES:out + MLA_QK_PAD].astype(BF16)

    subs = [slice(r0, r0 + SUB_PROJ) for r0 in range(0, TM_PROJ, SUB_PROJ)]
    pending = project(subs[0])
    for si, rows in enumerate(subs):
        current = pending
        if si + 1 < len(subs):
            pending = project(subs[si + 1])
        heads(rows, *current)


def _mla_proj(x, seq, an, wqa, qan, wqb, wkva, kvan, wkvb, gq, gk, tabs):
    m = x.shape[0]
    tm = TM_PROJ
    nblk = seq // tm
    row = lambda i: (i, 0)
    pos = lambda i: (i % nblk, 0)
    hq = MLA_HEADS * MLA_QK_PAD
    in_specs = [pl.BlockSpec((tm, D_MODEL), row), _resident(an.shape), _resident(wqa.shape),
                _resident(qan.shape), _resident(wqb.shape), _resident(wkva.shape),
                _resident(kvan.shape), _resident(wkvb.shape), _resident(gq.shape), _resident(gk.shape)]
    in_specs += [pl.BlockSpec((tm, LANES), pos)] * 2
    ones = jnp.ones((D_MODEL, LANES), BF16)
    in_specs.append(_resident(ones.shape))
    return pl.pallas_call(
        _mla_proj_kernel,
        out_shape=(jax.ShapeDtypeStruct((m, hq), BF16), jax.ShapeDtypeStruct((m, hq), BF16),
                   jax.ShapeDtypeStruct((m, MLA_HEADS * MLA_V), BF16)),
        grid=(m // tm,),
        in_specs=in_specs,
        out_specs=(pl.BlockSpec((tm, hq), row), pl.BlockSpec((tm, hq), row),
                   pl.BlockSpec((tm, MLA_HEADS * MLA_V), row)),
        compiler_params=_params(1),
        name="mla_proj",
    )(x, an, wqa, qan, wqb, wkva, kvan, wkvb, gq, gk, *tabs, ones)


def _mla_attn_kernel(q_ref, k_ref, v_ref, o_ref, v1_scr, *, seq, tq):
    hps = MLA_HEADS_PER_STEP
    for h in range(hps):
        v1_scr[h, :, :MLA_V] = v_ref[0, :, h * MLA_V:(h + 1) * MLA_V]
        v1_scr[h, :, MLA_V:] = jnp.ones((seq, MLA_V), BF16)

    def scores(h, t):
        cols = slice(h * MLA_QK_PAD, (h + 1) * MLA_QK_PAD)
        return lax.dot_general(q_ref[0, t * tq:(t + 1) * tq, cols], k_ref[0, :, cols], NT_DIMS,
                               preferred_element_type=F32)

    def finish(s, h, t):
        m = jnp.max(s, axis=-1, keepdims=True)
        p = jnp.exp2(s - m).astype(BF16)
        o = jnp.dot(p, v1_scr[h], preferred_element_type=F32)
        o_ref[0, t * tq:(t + 1) * tq, h * MLA_V:(h + 1) * MLA_V] = (
            o[:, :MLA_V] / o[:, MLA_V:]).astype(o_ref.dtype)

    tiles = [(h, t) for h in range(hps) for t in range(seq // tq)]
    queue = [scores(*ht) for ht in tiles[:PIPE_DEPTH]]
    for i, ht in enumerate(tiles):
        s = queue.pop(0)
        if i + PIPE_DEPTH < len(tiles):
            queue.append(scores(*tiles[i + PIPE_DEPTH]))
        finish(s, *ht)


def _mla_attn(q, k, v):
    b, seq, _ = q.shape
    hps = MLA_HEADS_PER_STEP
    head = lambda bi, hi: (bi, 0, hi)
    return pl.pallas_call(
        functools.partial(_mla_attn_kernel, seq=seq, tq=TQ_MLA),
        out_shape=jax.ShapeDtypeStruct((b, seq, MLA_HEADS * MLA_V), BF16),
        grid=(b, MLA_HEADS // hps),
        in_specs=[pl.BlockSpec((1, seq, hps * MLA_QK_PAD), head),
                  pl.BlockSpec((1, seq, hps * MLA_QK_PAD), head),
                  pl.BlockSpec((1, seq, hps * MLA_V), head)],
        out_specs=pl.BlockSpec((1, seq, hps * MLA_V), head),
        scratch_shapes=[pltpu.VMEM((hps, seq, 2 * MLA_V), BF16)],
        compiler_params=_params(2),
        name="mla_attn",
    )(q, k, v)


def _projected_chunks(x_ref, an_ref, w_ref):
    offsets = list(range(0, w_ref.shape[1], NORM_CHUNK))
    for r0 in range(0, TM_PROJ, SUB_PROJ):
        rows = slice(r0, r0 + SUB_PROJ)
        h = _rms(x_ref[rows, :], an_ref[...]).astype(BF16)
        chunk = lambda c0: jnp.dot(h, w_ref[:, c0:c0 + NORM_CHUNK], preferred_element_type=F32)
        queue = [chunk(c0) for c0 in offsets[:PROJ_DEPTH]]
        for i, c0 in enumerate(offsets):
            t = queue.pop(0)
            if i + PROJ_DEPTH < len(offsets):
                queue.append(chunk(offsets[i + PROJ_DEPTH]))
            yield rows, c0, t


def _normed_roped_halves(t, g, gmat, c, s1, s2):
    ms = jnp.dot((t * t).astype(BF16), gmat, preferred_element_type=F32)
    tn = t * lax.rsqrt(ms + NORM_EPS) * g
    for half in range(NORM_CHUNK // LANES):
        yield half * LANES, _rope(tn[:, half * LANES:(half + 1) * LANES], c, s1, s2)


def _swa_proj_kernel(x_ref, an_ref, w_ref, gq_ref, gk_ref, gmat_ref, c_ref, s1_ref, s2_ref,
                     q_out, k_out, v_out, *, nq, nk):
    gmat = gmat_ref[...]
    for rows, c0, t in _projected_chunks(x_ref, an_ref, w_ref):
        if c0 >= nq + nk:
            v_out[rows, c0 - nq - nk:c0 - nq - nk + NORM_CHUNK] = t.astype(BF16)
            continue
        out, g, lo = (q_out, gq_ref, c0) if c0 < nq else (k_out, gk_ref, c0 - nq)
        tabs = (c_ref[rows, :], s1_ref[rows, :], s2_ref[rows, :])
        for off, val in _normed_roped_halves(t, g[...], gmat, *tabs):
            out[rows, lo + off:lo + off + LANES] = val.astype(BF16)


def _swa_proj(x, seq, an, w, gq, gk, gmat, tabs, nq, nk):
    m = x.shape[0]
    tm = TM_PROJ
    nblk = seq // tm
    nv = w.shape[1] - nq - nk
    row = lambda i: (i, 0)
    pos = lambda i: (i % nblk, 0)
    in_specs = [pl.BlockSpec((tm, D_MODEL), row), _resident(an.shape), _resident(w.shape),
                _resident(gq.shape), _resident(gk.shape), _resident(gmat.shape)]
    in_specs += [pl.BlockSpec((tm, LANES), pos)] * 3
    return pl.pallas_call(
        functools.partial(_swa_proj_kernel, nq=nq, nk=nk),
        out_shape=(jax.ShapeDtypeStruct((m, nq), BF16), jax.ShapeDtypeStruct((m, nk), BF16),
                   jax.ShapeDtypeStruct((m, nv), BF16)),
        grid=(m // tm,),
        in_specs=in_specs,
        out_specs=(pl.BlockSpec((tm, nq), row), pl.BlockSpec((tm, nk), row), pl.BlockSpec((tm, nv), row)),
        compiler_params=_params(1),
        name="swa_proj",
    )(x, an, w, gq, gk, gmat, *tabs)


def _dil_proj_kernel(x_ref, an_ref, w_ref, gq_ref, gk_ref, gmat_ref, c_ref, s1_ref, s2_ref, *refs):
    outs, stage = refs[:3 * DIL_GROUPS], refs[3 * DIL_GROUPS]
    gmat = gmat_ref[...]
    nq = DIL_GROUPS * DIL_GROUP_WIDTH
    slabs_per_group = DIL_GROUP_WIDTH // LANES
    for rows, c0, t in _projected_chunks(x_ref, an_ref, w_ref):
        r0 = rows.start
        sub = r0 // SUB_PROJ
        tensor, lo0 = c0 // nq, c0 % nq

        def emit(tensor, lo, val, sub=sub, r0=r0):
            gi, c = lo // DIL_GROUP_WIDTH, (lo % DIL_GROUP_WIDTH) // LANES
            dil = DIL_CONFIGS[gi][1]
            out = outs[3 * gi + tensor]
            cols = slice(c * LANES, (c + 1) * LANES)
            if dil == 1:
                out[0, 0, r0:r0 + SUB_PROJ, cols] = val.astype(BF16)
                return
            slot = ((sub % STAGE_SETS * 3 + tensor) * (DIL_GROUPS - 1) + gi - 1) * slabs_per_group + c
            stage[slot] = val
            n = SUB_PROJ // dil
            for r in range(dil):
                out[0, r, r0 // dil:r0 // dil + n, cols] = stage[slot, pl.ds(r, n, stride=dil), :].astype(BF16)

        if tensor == 2:
            for off in range(0, NORM_CHUNK, LANES):
                emit(2, lo0 + off, t[:, off:off + LANES])
            continue
        tabs = (c_ref[rows, :], s1_ref[rows, :], s2_ref[rows, :])
        for off, val in _normed_roped_halves(t, (gq_ref, gk_ref)[tensor][...], gmat, *tabs):
            emit(tensor, lo0 + off, val)


def _dil_proj(x, b, seq, an, w, gq, gk, gmat, tabs):
    m = x.shape[0]
    tm = TM_PROJ
    nblk = seq // tm
    row = lambda i: (i, 0)
    pos = lambda i: (i % nblk, 0)
    in_specs = [pl.BlockSpec((tm, D_MODEL), row), _resident(an.shape), _resident(w.shape),
                _resident(gq.shape), _resident(gk.shape), _resident(gmat.shape)]
    in_specs += [pl.BlockSpec((tm, LANES), pos)] * 3
    out_shape, out_specs = [], []
    for _, dil in DIL_CONFIGS:
        for _ in range(3):
            out_shape.append(jax.ShapeDtypeStruct((b, dil, seq // dil, DIL_GROUP_WIDTH), BF16))
            out_specs.append(pl.BlockSpec((1, dil, tm // dil, DIL_GROUP_WIDTH),
                                          lambda i: (i // nblk, 0, i % nblk, 0)))
    n_slots = STAGE_SETS * 3 * (DIL_GROUPS - 1) * (DIL_GROUP_WIDTH // LANES)
    return pl.pallas_call(
        _dil_proj_kernel,
        out_shape=tuple(out_shape),
        grid=(m // tm,),
        in_specs=in_specs,
        out_specs=tuple(out_specs),
        scratch_shapes=[pltpu.VMEM((n_slots, SUB_PROJ, LANES), F32)],
        compiler_params=_params(1),
        name="dil_proj",
    )(x, an, w, gq, gk, gmat, *tabs)


def _band_bias_t(win, tq, off, half):
    dd = (lax.broadcasted_iota(jnp.int32, (win, tq), 0)
          - lax.broadcasted_iota(jnp.int32, (win, tq), 1))
    return jnp.where(jnp.abs(dd + off) <= half, 0.0, NEG_BIG).astype(F32)


def _fill_vt(vt_scr, v_block, head0, cols):
    d = HEAD_DIM
    vt = v_block.astype(F32).T.astype(BF16)
    for e in range(LANES // d):
        lo = (head0 + e) * VT_ROWS
        vt_scr[lo:lo + d, cols] = vt[e * d:(e + 1) * d, :]
        vt_scr[lo + d:lo + VT_ROWS, cols] = jnp.ones((VT_ROWS - d, vt.shape[1]), BF16)


def _probs_t(st, sk2):
    m = jnp.max(st, axis=0, keepdims=True)
    if sk2 is not None:
        m = jnp.maximum(m, sk2)
    return jnp.exp2(st - m).astype(BF16), m


def _swa_attn_kernel(q_ref, k_ref, v_ref, sink_ref, o_ref, vt_scr, *, length, tq, win, half, q_heads, group):
    d = HEAD_DIM
    chunk = 4 * LANES
    for j in range(q_heads // group * d // LANES):
        for c0 in range(0, length, chunk):
            _fill_vt(vt_scr, v_ref[0, c0:c0 + chunk, j * LANES:(j + 1) * LANES], 2 * j, slice(c0, c0 + chunk))

    def body(i, carry):
        q0s, starts, biases = [], [], []
        for j in range(SWA_TILES_PER_TRIP):
            q0 = pl.multiple_of((i * SWA_TILES_PER_TRIP + j) * tq, tq)
            start = pl.multiple_of(jnp.clip(q0 - half, 0, length - win), LANES)
            q0s.append(q0)
            starts.append(start)
            biases.append(_band_bias_t(win, tq, start - q0, half))

        def scores(item):
            j, a = item
            kg = k_ref[0, pl.ds(starts[j], win), (a // group) * d:(a // group + 1) * d]
            qa = q_ref[0, pl.ds(q0s[j], tq), a * d:(a + 1) * d]
            return lax.dot_general(kg, qa, NT_DIMS, preferred_element_type=F32) + biases[j]

        items = [(j, a) for j in range(SWA_TILES_PER_TRIP) for a in range(q_heads)]
        queue = [scores(item) for item in items[:PIPE_DEPTH]]
        pair = []
        for n, (j, a) in enumerate(items):
            st = queue.pop(0)
            if n + PIPE_DEPTH < len(items):
                queue.append(scores(items[n + PIPE_DEPTH]))
            sk2 = sink_ref[:, a:a + 1]
            pt, m = _probs_t(st, sk2)
            g = a // group
            o2t = jnp.dot(vt_scr[g * VT_ROWS:(g + 1) * VT_ROWS, pl.ds(starts[j], win)], pt,
                          preferred_element_type=F32)
            pair.append(o2t[:d, :] / (o2t[d:d + 1, :] + jnp.exp2(sk2 - m)))
            if len(pair) == LANES // d:
                o_ref[0, pl.ds(q0s[j], tq), (a - 1) * d:(a + 1) * d] = (
                    jnp.concatenate(pair, axis=0).T.astype(o_ref.dtype))
                pair = []
        return carry

    lax.fori_loop(0, length // (tq * SWA_TILES_PER_TRIP), body, 0)


def _swa_attn(q, k, v, sink):
    b, length, nq = q.shape
    nk = k.shape[-1]
    tq, half = TQ_BAND, SWA_HALF_WINDOW
    whole = lambda bi: (bi, 0, 0)
    return pl.pallas_call(
        functools.partial(_swa_attn_kernel, length=length, tq=tq, win=tq + 2 * half, half=half,
                          q_heads=SWA_Q_HEADS, group=SWA_Q_HEADS // SWA_KV_HEADS),
        out_shape=jax.ShapeDtypeStruct((b, length, nq), BF16),
        grid=(b,),
        in_specs=[pl.BlockSpec((1, length, nq), whole), pl.BlockSpec((1, length, nk), whole),
                  pl.BlockSpec((1, length, nk), whole), _resident(sink.shape)],
        out_specs=pl.BlockSpec((1, length, nq), whole),
        scratch_shapes=[pltpu.VMEM((SWA_KV_HEADS * VT_ROWS, length), BF16)],
        compiler_params=_params(1),
        name="swa_attn",
    )(q, k, v, sink)


def _dil_attn_kernel(*refs, seq, tq):
    qkv, o_ref = refs[:3 * DIL_GROUPS], refs[3 * DIL_GROUPS]
    acc_scr, lse_scr, vt_scr = refs[3 * DIL_GROUPS + 1:]
    d, heads = DIL_HEAD_DIM, DIL_HEADS_PER_GROUP
    slabs = DIL_GROUP_WIDTH // LANES

    for gi, (window, dil) in enumerate(DIL_CONFIGS):
        q_ref, k_ref, v_ref = qkv[3 * gi:3 * gi + 3]
        length = seq // dil
        half = window // (2 * dil)
        t_q = min(tq, length)
        win = min(length, t_q + 2 * LANES)
        n_tiles = length // t_q

        def fill(r, carry, v_ref=v_ref, length=length):
            cols = pl.ds(pl.multiple_of(r * length, LANES), length)
            for j in range(slabs):
                _fill_vt(vt_scr, v_ref[0, r, :, j * LANES:(j + 1) * LANES], 2 * j, cols)
            return carry

        lax.fori_loop(0, dil, fill, 0)

        per_trip = DIL_SUBSEQ_PER_TRIP if n_tiles == 1 else DIL_TILES_PER_TRIP

        def tile(idx, carry, gi=gi, dil=dil, length=length, half=half, t_q=t_q, win=win,
                 n_tiles=n_tiles, per_trip=per_trip, q_ref=q_ref, k_ref=k_ref):
            residues, q0s, starts, biases = [], [], [], []
            for j in range(per_trip):
                unit = idx * per_trip + j
                q0 = pl.multiple_of((unit % n_tiles) * t_q, t_q)
                start = pl.multiple_of(jnp.clip(q0 - LANES, 0, length - win), LANES)
                residues.append(unit // n_tiles)
                q0s.append(q0)
                starts.append(start)
                biases.append(_band_bias_t(win, t_q, start - q0, half) if j == 0 or n_tiles > 1 else biases[0])
            residue = lambda j: residues[j]

            def scores(item):
                j, hh = item
                qa = q_ref[0, residue(j), pl.ds(q0s[j], t_q), hh * d:(hh + 1) * d]
                kg = k_ref[0, residue(j), pl.ds(starts[j], win), hh * d:(hh + 1) * d]
                return lax.dot_general(kg, qa, NT_DIMS, preferred_element_type=F32) + biases[j]

            def merge(j, c, o_new, l_new):
                rows = (pl.ds(q0s[j] * dil + residue(j), t_q, stride=dil) if dil > 1
                        else pl.ds(q0s[j], t_q))
                if gi == 0:
                    acc_scr[c, rows, :] = o_new
                    lse_scr[c, rows, :] = l_new
                    return
                l_old = lse_scr[c, rows, :]
                mx = jnp.maximum(l_old, l_new)
                wa = jnp.exp2(l_old - mx)
                wb = jnp.exp2(l_new - mx)
                acc_scr[c, rows, :] = (wa * acc_scr[c, rows, :] + wb * o_new) / (wa + wb)
                if gi + 1 < DIL_GROUPS:
                    lse_scr[c, rows, :] = mx + jnp.log2(wa + wb)

            items = [(j, hh) for j in range(per_trip) for hh in range(heads)]
            queue = [scores(item) for item in items[:PIPE_DEPTH]]
            o_pair, l_pair = [], []
            for i, (j, hh) in enumerate(items):
                st = queue.pop(0)
                if i + PIPE_DEPTH < len(items):
                    queue.append(scores(items[i + PIPE_DEPTH]))
                pt, m = _probs_t(st, None)
                vcols = pl.ds(pl.multiple_of(residue(j) * length + starts[j], LANES), win)
                o2t = jnp.dot(vt_scr[hh * VT_ROWS:(hh + 1) * VT_ROWS, vcols], pt,
                              preferred_element_type=F32)
                den = o2t[d:d + 1, :]
                o_pair.append(o2t[:d, :] / den)
                l_pair.append(jnp.broadcast_to(m + jnp.log2(den), (d, t_q)))
                if len(o_pair) == LANES // d:
                    merge(j, hh // 2, jnp.concatenate(o_pair, axis=0).T, jnp.concatenate(l_pair, axis=0).T)
                    o_pair, l_pair = [], []
            return carry

        lax.fori_loop(0, dil * n_tiles // per_trip, tile, 0)

    for c in range(slabs):
        o_ref[0, :, c * LANES:(c + 1) * LANES] = acc_scr[c].astype(o_ref.dtype)


def _dil_attn(qkv9, seq):
    b = qkv9[0].shape[0]
    in_specs = [pl.BlockSpec((1,) + t.shape[1:], lambda bi: (bi, 0, 0, 0)) for t in qkv9]
    slabs = DIL_GROUP_WIDTH // LANES
    return pl.pallas_call(
        functools.partial(_dil_attn_kernel, seq=seq, tq=TQ_BAND),
        out_shape=jax.ShapeDtypeStruct((b, seq, DIL_GROUP_WIDTH), BF16),
        grid=(b,),
        in_specs=in_specs,
        out_specs=pl.BlockSpec((1, seq, DIL_GROUP_WIDTH), lambda bi: (bi, 0, 0)),
        scratch_shapes=[pltpu.VMEM((slabs, seq, LANES), F32), pltpu.VMEM((slabs, seq, LANES), F32),
                        pltpu.VMEM((DIL_HEADS_PER_GROUP * VT_ROWS, seq), BF16)],
        compiler_params=_params(1),
        name="dil_attn",
    )(*qkv9)


def _out_ffn_kernel(x_ref, o_ref, wo_ref, fn_ref, wg_ref, wu_ref, wd_ref, y_ref):
    subs = [slice(r0, r0 + SUB_FFN) for r0 in range(0, TM_FFN, SUB_FFN)]
    x1s = [x_ref[rows, :] + jnp.dot(o_ref[rows, :], wo_ref[...], preferred_element_type=F32) for rows in subs]
    hs = [_rms(x1, fn_ref[...]).astype(BF16) for x1 in x1s]
    gate_up = [(jnp.dot(h, wg_ref[...], preferred_element_type=F32),
                jnp.dot(h, wu_ref[...], preferred_element_type=F32)) for h in hs]
    acts = [(gate * jax.nn.sigmoid(gate) * up).astype(BF16) for gate, up in gate_up]
    for rows, x1, act in zip(subs, x1s, acts):
        y_ref[rows, :] = x1 + jnp.dot(act, wd_ref[...], preferred_element_type=F32)


def _out_ffn(x, o, wo, fn, wg, wu, wd):
    m = x.shape[0]
    tm = TM_FFN
    row = lambda i: (i, 0)
    in_specs = [pl.BlockSpec((tm, D_MODEL), row), pl.BlockSpec((tm, o.shape[1]), row),
                _resident(wo.shape), _resident(fn.shape), _resident(wg.shape), _resident(wu.shape),
                _resident(wd.shape)]
    return pl.pallas_call(
        _out_ffn_kernel,
        out_shape=jax.ShapeDtypeStruct((m, D_MODEL), F32),
        grid=(m // tm,),
        in_specs=in_specs,
        out_specs=pl.BlockSpec((tm, D_MODEL), row),
        compiler_params=_params(1),
        name="out_ffn",
    )(x, o, wo, fn, wg, wu, wd)


def _rope_tables(seq, dim, lanes):
    inv = 1.0 / (ROPE_THETA ** (jnp.arange(0, dim, 2, dtype=F32) / dim))
    ang = jnp.arange(seq, dtype=F32)[:, None] * inv[None, :]
    cos, sin = jnp.cos(ang), jnp.sin(ang)
    zero = jnp.zeros_like(sin)
    tabs = (jnp.concatenate([cos, cos], -1), jnp.concatenate([-sin, zero], -1),
            jnp.concatenate([zero, sin], -1))
    return tuple(jnp.tile(t, (1, lanes // dim)) for t in tabs)


def _row(v):
    return v.reshape(1, -1).astype(F32)


def _with_rope_partner(t):
    half = MLA_ROPE // 2
    zeros = jnp.zeros(t.shape[:-1] + (LANES - MLA_ROPE,), t.dtype)
    partner = jnp.concatenate([t[..., half:], t[..., :half]], -1)
    return jnp.concatenate([t, zeros, partner, zeros], -1)


def _head_gain(g, reps, scale=1.0):
    return jnp.tile(g.astype(F32) * scale, reps).reshape(1, -1)


def _trunk(x, attn_norm, ffn_norm, w_gate, w_up, w_down,
           mla_wq_a, mla_q_a_norm, mla_wq_b, mla_wkv_a, mla_kv_a_norm, mla_wkv_b,
           mla_q_norm, mla_k_norm, mla_wo,
           swa_wqkv, swa_q_norm, swa_k_norm, swa_sink, swa_wo,
           dil_wqkv, dil_q_norm, dil_k_norm, dil_wo):
    b, seq, _ = x.shape
    m = b * seq
    xf = x.reshape(m, D_MODEL)
    tabs64 = _rope_tables(seq, HEAD_DIM, LANES)
    cos, msin, psin = _rope_tables(seq, MLA_ROPE, MLA_ROPE)
    pad = ((0, 0), (0, LANES - MLA_ROPE))
    tabs_mla = (jnp.pad(cos, pad), jnp.pad(msin + psin, pad))
    blk = jnp.arange(NORM_CHUNK) // HEAD_DIM
    gmat = ((blk[:, None] == blk[None, :]).astype(F32) / HEAD_DIM).astype(BF16)
    reps = NORM_CHUNK // HEAD_DIM

    for i in range(DEPTH):
        kind, j = i % N_MIXERS, i // N_MIXERS
        an = _row(attn_norm[i])
        if kind == 0:
            wqb = mla_wq_b[j].reshape(MLA_Q_LORA, MLA_HEADS, MLA_QK)
            wqb = jnp.concatenate([wqb[..., :MLA_NOPE], _with_rope_partner(wqb[..., MLA_NOPE:])], -1)
            wkva = jnp.concatenate([mla_wkv_a[j][:, :MLA_KV_LORA],
                                    _with_rope_partner(mla_wkv_a[j][:, MLA_KV_LORA:])], -1)
            gains = [jnp.concatenate([g[:MLA_NOPE], _with_rope_partner(g[MLA_NOPE:])])
                     for g in (mla_q_norm[j], mla_k_norm[j])]
            q, k, v = _mla_proj(
                xf, seq, an, mla_wq_a[j].astype(BF16), _row(mla_q_a_norm[j]),
                wqb.reshape(MLA_Q_LORA, -1).astype(BF16), wkva.astype(BF16), _row(mla_kv_a_norm[j]),
                mla_wkv_b[j].astype(BF16), _row(gains[0]), _row(gains[1]), tabs_mla)
            o = _mla_attn(q.reshape(b, seq, -1), k.reshape(b, seq, -1), v.reshape(b, seq, -1))
            wo = mla_wo[j]
        elif kind == 1:
            nq, nk = SWA_Q_HEADS * SWA_HEAD_DIM, SWA_KV_HEADS * SWA_HEAD_DIM
            q, k, v = _swa_proj(xf, seq, an, swa_wqkv[j].astype(BF16),
                                _head_gain(swa_q_norm[j], reps, SWA_HEAD_DIM ** -0.5 * LOG2E),
                                _head_gain(swa_k_norm[j], reps), gmat, tabs64, nq, nk)
            o = _swa_attn(q.reshape(b, seq, nq), k.reshape(b, seq, nk), v.reshape(b, seq, nk),
                          _row(swa_sink[j]) * LOG2E)
            wo = swa_wo[j]
        else:
            qkv9 = _dil_proj(xf, b, seq, an, dil_wqkv[j].astype(BF16),
                             _head_gain(dil_q_norm[j], reps, DIL_HEAD_DIM ** -0.5 * LOG2E),
                             _head_gain(dil_k_norm[j], reps), gmat, tabs64)
            o = _dil_attn(qkv9, seq)
            wo = dil_wo[j]
        xf = _out_ffn(xf, o.reshape(m, -1), wo.astype(BF16), _row(ffn_norm[i]), w_gate[i].astype(BF16),
                      w_up[i].astype(BF16), w_down[i].astype(BF16))
    return xf.reshape(b, seq, D_MODEL)


def kernel(x_prompt, x_sample, attn_norm, ffn_norm, w_gate, w_up, w_down, mla_wq_a, mla_q_a_norm, mla_wq_b, mla_wkv_a, mla_kv_a_norm, mla_wkv_b, mla_q_norm, mla_k_norm, mla_wo, swa_wqkv, swa_q_norm, swa_k_norm, swa_sink, swa_wo, dil_wqkv, dil_q_norm, dil_k_norm, dil_wo):
    weights = (attn_norm, ffn_norm, w_gate, w_up, w_down,
               mla_wq_a, mla_q_a_norm, mla_wq_b, mla_wkv_a, mla_kv_a_norm, mla_wkv_b,
               mla_q_norm, mla_k_norm, mla_wo,
               swa_wqkv, swa_q_norm, swa_k_norm, swa_sink, swa_wo,
               dil_wqkv, dil_q_norm, dil_k_norm, dil_wo)
    return (_trunk(x_prompt, *weights), _trunk(x_sample, *weights))
```
